```python
import jax, jax.numpy as jnp
from jax import lax
import numpy as np

D_MODEL = 1024
BATCH = 16
SEQ = 2048
DEPTH = 2

GRID_W = 64
CTX_LEN = 256
N_EVEN = (DEPTH + 1) // 2
N_ODD = DEPTH // 2
N_MOD = 6
NORM_EPS = 1e-6
CONV_CH = D_MODEL // 2
CONV_WIDTH = 31
FOURIER_CH = D_MODEL // 2
FOURIER_GROUPS = 4
FOURIER_GROUP_CH = FOURIER_CH // FOURIER_GROUPS
AB_IN = 2 * CONV_CH + FOURIER_CH
AB_MIX = CONV_CH + FOURIER_CH
MLA_HEADS = 16
QK_NOPE = 64
QK_ROPE = 32
V_HEAD = 64
QK_HEAD = QK_NOPE + QK_ROPE
Q_LORA = 256
KV_LORA = 128
MLA_IN = Q_LORA + KV_LORA + QK_ROPE
ROPE_BASE = 10000.0
ROPE_FREQS_PER_AXIS = QK_ROPE // 4
ATTN_SCALE = QK_HEAD ** -0.5
Q_BLOCK = 128
N_EXPERTS = 16
EXPERT_FF = 1024
EC_CAPACITY_FACTOR = 2

kernel_name = 'hybrid_conv_fourier_mla_ecmoe_prefix_dit'


def rmsnorm(x, g):
    xf = x.astype(jnp.float32)
    y = xf * lax.rsqrt(jnp.mean(xf * xf, axis=-1, keepdims=True) + NORM_EPS)
    return (y * g.astype(jnp.float32)).astype(x.dtype)


def layernorm(x, g, b):
    xf = x.astype(jnp.float32)
    mu = jnp.mean(xf, axis=-1, keepdims=True)
    var = jnp.mean(jnp.square(xf - mu), axis=-1, keepdims=True)
    y = (xf - mu) * lax.rsqrt(var + NORM_EPS)
    return (y * g.astype(jnp.float32) + b.astype(jnp.float32)).astype(x.dtype)


def adaln(cvec, w, b):
    m = (jax.nn.silu(cvec) @ w + b)[:, None, :]
    return jnp.split(m, N_MOD, axis=-1)


def modulate(h, shift, scale):
    return h * (1 + scale) + shift


def conv_fourier_mixer(h, w_in, conv_w, conv_b, ln_g, ln_b, w_out):
    n, L, _ = h.shape
    z = h @ w_in
    a_val, a_gate, u = jnp.split(z, [CONV_CH, 2 * CONV_CH], axis=-1)
    a = a_val * jax.nn.sigmoid(a_gate)
    a = lax.conv_general_dilated(a, conv_w[:, None, :], window_strides=(1,),
                                 padding=[(CONV_WIDTH // 2, CONV_WIDTH // 2)],
                                 dimension_numbers=('NWC', 'WIO', 'NWC'),
                                 feature_group_count=CONV_CH) + conv_b
    a = jax.nn.silu(layernorm(a, ln_g, ln_b))
    ug = u.astype(jnp.float32).reshape(n, L, FOURIER_GROUPS, FOURIER_GROUP_CH)
    f = jnp.fft.fft2(ug, axes=(1, 3), norm='ortho').real.reshape(n, L, FOURIER_CH).astype(h.dtype)
    return jnp.concatenate([a, f], axis=-1) @ w_out


def axial_rope(L):
    rows = L // GRID_W
    row = jnp.repeat(jnp.arange(rows, dtype=jnp.float32), GRID_W)
    col = jnp.tile(jnp.arange(GRID_W, dtype=jnp.float32), rows)
    inv_freq = ROPE_BASE ** (-jnp.arange(ROPE_FREQS_PER_AXIS, dtype=jnp.float32) / ROPE_FREQS_PER_AXIS)
    ang = jnp.concatenate([row[:, None] * inv_freq, col[:, None] * inv_freq], axis=-1)
    return jnp.cos(ang), jnp.sin(ang)


def apply_rope(x, cos, sin):
    xf = x.astype(jnp.float32).reshape(x.shape[:-1] + (QK_ROPE // 2, 2))
    xe, xo = xf[..., 0], xf[..., 1]
    out = jnp.stack([xe * cos - xo * sin, xe * sin + xo * cos], axis=-1)
    return out.reshape(x.shape).astype(x.dtype)


def mla_queries(cq, q_norm_g, w_uq, rope):
    n, L, _ = cq.shape
    q = (rmsnorm(cq, q_norm_g) @ w_uq).reshape(n, L, MLA_HEADS, QK_HEAD)
    q_nope, q_rope = q[..., :QK_NOPE], q[..., QK_NOPE:]
    if rope is not None:
        q_rope = apply_rope(q_rope, rope[0][:, None, :], rope[1][:, None, :])
    return jnp.concatenate([q_nope, q_rope], axis=-1)


def mla_keys_values(ckv, k_rope, kv_norm_g, w_ukv, rope):
    n, L, _ = ckv.shape
    kv = (rmsnorm(ckv, kv_norm_g) @ w_ukv).reshape(n, L, MLA_HEADS, QK_NOPE + V_HEAD)
    k_nope, v = kv[..., :QK_NOPE], kv[..., QK_NOPE:]
    if rope is not None:
        k_rope = apply_rope(k_rope, rope[0], rope[1])
    k_rope = jnp.broadcast_to(k_rope[:, :, None, :], (n, L, MLA_HEADS, QK_ROPE))
    return jnp.concatenate([k_nope, k_rope], axis=-1), v


def attend(q, k, v):
    s = jnp.einsum('bqhd,bkhd->bhqk', q, k, preferred_element_type=jnp.float32) * ATTN_SCALE
    p = jax.nn.softmax(s, axis=-1).astype(v.dtype)
    return jnp.einsum('bhqk,bkhd->bqhd', p, v)


def blocked_attend(q, k, v):
    n, L, H, dq = q.shape
    nb = L // Q_BLOCK
    qb = q.reshape(n, nb, Q_BLOCK, H, dq).transpose(1, 0, 2, 3, 4)
    ob = lax.map(lambda qi: attend(qi, k, v), qb)
    return ob.transpose(1, 0, 2, 3, 4).reshape(n, L, H * V_HEAD)


def ec_moe(h, w_router, w1, w3, w2):
    n, L, D = h.shape
    cap = EC_CAPACITY_FACTOR * L // N_EXPERTS
    logits = jnp.einsum('bld,de->ble', h, w_router, preferred_element_type=jnp.float32)
    aff = jax.nn.softmax(logits, axis=-1)
    gate, idx = lax.top_k(jnp.swapaxes(aff, 1, 2), cap)
    xg = jax.vmap(lambda hb, ib: hb[ib])(h, idx)
    hid = jax.nn.silu(jnp.einsum('becd,edf->becf', xg, w1)) * jnp.einsum('becd,edf->becf', xg, w3)
    out = jnp.einsum('becf,efd->becd', hid, w2) * gate[..., None].astype(h.dtype)
    return jax.vmap(lambda ib, ob: jnp.zeros((L, D), h.dtype).at[ib.reshape(-1)].add(ob.reshape(-1, D)))(idx, out)


def setup_inputs(seed: int = 0) -> dict:
    key = jax.random.key(seed)
    ks = jax.random.split(key, 32)
    f32 = jnp.float32
    D = D_MODEL

    def nrm(k, shape, fan_in, gain=1.0):
        return gain * fan_in ** -0.5 * jax.random.normal(k, shape, f32)

    def gains(k, shape):
        return 1.0 + 0.05 * jax.random.normal(k, shape, f32)

    def small(k, shape):
        return 0.02 * jax.random.normal(k, shape, f32)

    return {
        'x': jax.random.normal(ks[0], (BATCH, SEQ, D), f32),
        'c': jax.random.normal(ks[1], (BATCH, D), f32),
        'ctx': jax.random.normal(ks[2], (BATCH, CTX_LEN, D), f32),
        'c_ctx': jax.random.normal(ks[3], (D,), f32),
        'mod_w': nrm(ks[4], (DEPTH, D, N_MOD * D), D, 0.5),
        'mod_b': small(ks[5], (DEPTH, N_MOD * D)),
        'norm1_g': gains(ks[6], (DEPTH, D)),
        'norm2_g': gains(ks[7], (DEPTH, D)),
        'ab_w_in': nrm(ks[8], (N_EVEN, D, AB_IN), D),
        'ab_conv_w': nrm(ks[9], (N_EVEN, CONV_WIDTH, CONV_CH), CONV_WIDTH),
        'ab_conv_b': small(ks[10], (N_EVEN, CONV_CH)),
        'ab_ln_g': gains(ks[11], (N_EVEN, CONV_CH)),
        'ab_ln_b': small(ks[12], (N_EVEN, CONV_CH)),
        'ab_w_out': nrm(ks[13], (N_EVEN, AB_MIX, D), AB_MIX),
        'mla_w_in': nrm(ks[14], (N_ODD, D, MLA_IN), D),
        'mla_q_norm_g': gains(ks[15], (N_ODD, Q_LORA)),
        'mla_kv_norm_g': gains(ks[16], (N_ODD, KV_LORA)),
        'mla_w_uq': nrm(ks[17], (N_ODD, Q_LORA, MLA_HEADS * QK_HEAD), Q_LORA),
        'mla_w_ukv': nrm(ks[18], (N_ODD, KV_LORA, MLA_HEADS * (QK_NOPE + V_HEAD)), KV_LORA),
        'mla_w_o': nrm(ks[19], (N_ODD, MLA_HEADS * V_HEAD, D), MLA_HEADS * V_HEAD),
        'moe_w_router': nrm(ks[20], (DEPTH, D, N_EXPERTS), D),
        'moe_w1': nrm(ks[21], (DEPTH, N_EXPERTS, D, EXPERT_FF), D),
        'moe_w3': nrm(ks[22], (DEPTH, N_EXPERTS, D, EXPERT_FF), D),
        'moe_w2': nrm(ks[23], (DEPTH, N_EXPERTS, EXPERT_FF, D), EXPERT_FF),
        'final_g': gains(ks[24], (D,)),
    }


def reference(x, c, ctx, c_ctx, mod_w, mod_b, norm1_g, norm2_g, ab_w_in, ab_conv_w, ab_conv_b,
              ab_ln_g, ab_ln_b, ab_w_out, mla_w_in, mla_q_norm_g, mla_kv_norm_g, mla_w_uq,
              mla_w_ukv, mla_w_o, moe_w_router, moe_w1, moe_w3, moe_w2, final_g):
    n, L, _ = x.shape
    rope = axial_rope(L)
    x_lat, x_ctx = x, ctx
    for i in range(DEPTH):
        ctx_out = i < DEPTH - 1
        j = i // 2
        sh1, sc1, g1, sh2, sc2, g2 = adaln(c, mod_w[i], mod_b[i])
        csh1, csc1, cg1, csh2, csc2, cg2 = adaln(c_ctx[None, :], mod_w[i], mod_b[i])
        u_lat = modulate(rmsnorm(x_lat, norm1_g[i]), sh1, sc1)
        u_ctx = modulate(rmsnorm(x_ctx, norm1_g[i]), csh1, csc1)
        if i % 2 == 0:
            ab = (ab_w_in[j], ab_conv_w[j], ab_conv_b[j], ab_ln_g[j], ab_ln_b[j], ab_w_out[j])
            x_lat = x_lat + g1 * conv_fourier_mixer(u_lat, *ab)
            if ctx_out:
                x_ctx = x_ctx + cg1 * conv_fourier_mixer(u_ctx, *ab)
        else:
            split = [Q_LORA, Q_LORA + KV_LORA]
            cq_lat, ckv_lat, kr_lat = jnp.split(u_lat @ mla_w_in[j], split, axis=-1)
            cq_ctx, ckv_ctx, kr_ctx = jnp.split(u_ctx @ mla_w_in[j], split, axis=-1)
            k_lat, v_lat = mla_keys_values(ckv_lat, kr_lat, mla_kv_norm_g[j], mla_w_ukv[j], rope)
            k_ctx, v_ctx = mla_keys_values(ckv_ctx, kr_ctx, mla_kv_norm_g[j], mla_w_ukv[j], None)
            q_lat = mla_queries(cq_lat, mla_q_norm_g[j], mla_w_uq[j], rope)
            k_all = jnp.concatenate([k_ctx, k_lat], axis=1)
            v_all = jnp.concatenate([v_ctx, v_lat], axis=1)
            x_lat = x_lat + g1 * (blocked_attend(q_lat, k_all, v_all) @ mla_w_o[j])
            if ctx_out:
                q_ctx = mla_queries(cq_ctx, mla_q_norm_g[j], mla_w_uq[j], None)
                o_ctx = attend(q_ctx, k_ctx, v_ctx).reshape(n, x_ctx.shape[1], MLA_HEADS * V_HEAD)
                x_ctx = x_ctx + cg1 * (o_ctx @ mla_w_o[j])
        moe = (moe_w_router[i], moe_w1[i], moe_w3[i], moe_w2[i])
        u_lat = modulate(rmsnorm(x_lat, norm2_g[i]), sh2, sc2)
        x_lat = x_lat + g2 * ec_moe(u_lat, *moe)
        if ctx_out:
            u_ctx = modulate(rmsnorm(x_ctx, norm2_g[i]), csh2, csc2)
            x_ctx = x_ctx + cg2 * ec_moe(u_ctx, *moe)
    return rmsnorm(x_lat, final_g)
```

```python
import functools

import numpy as np
import jax
import jax.numpy as jnp
from jax import lax
from jax.experimental import pallas as pl
from jax.experimental.pallas import tpu as pltpu

F32 = jnp.float32
BF16 = jnp.bfloat16

NORM_EPS = 1e-6
N_MOD = 6
GRID_W = 64
CONV_WIDTH = 31
CONV_HALO = 16
CONV_TAP_GROUPS = 4
FOURIER_GROUPS = 4
MLA_HEADS = 16
QK_NOPE = 64
QK_ROPE = 32
V_HEAD = 64
QK_HEAD = QK_NOPE + QK_ROPE
HEAD_PAD = 128
Q_LORA = 256
KV_LORA = 128
ROPE_BASE = 10000.0
SCORE_SCALE = float(QK_HEAD ** -0.5 * np.log2(np.e))
KEY_CHUNK = 512
N_EXPERTS = 16
EC_CAPACITY_FACTOR = 2
MOD_ROWS = 24
LANES = 128
VMEM_LIMIT = 56 * 1024 * 1024


def _cparams(*sem):
    return pltpu.CompilerParams(dimension_semantics=sem, vmem_limit_bytes=VMEM_LIMIT)


def _dot(a, b):
    return jnp.dot(a, b, preferred_element_type=F32)


def _dot_nt(a, b):
    return lax.dot_general(a, b, (((1,), (1,)), ((), ())), preferred_element_type=F32)


def _sigmoid(x):
    return 1.0 / (1.0 + jnp.exp(-x))


def _norm_modulate(x, g, shift, scale):
    ms = jnp.mean(x * x, axis=-1, keepdims=True)
    y = x * lax.rsqrt(ms + NORM_EPS) * g
    return y * (1.0 + scale) + shift


def _mod_kernel(c_ref, w_ref, b_ref, o_ref):
    c = c_ref[...]
    s = (c * _sigmoid(c)).astype(BF16)
    o_ref[0] = _dot(s, w_ref[0].astype(BF16)) + b_ref[0]


def _modulation(cvecs, mod_w, mod_b):
    depth, d, nd = mod_w.shape
    tn = 1024
    return pl.pallas_call(
        _mod_kernel,
        grid=(depth, nd // tn),
        in_specs=[
            pl.BlockSpec((MOD_ROWS, d), lambda i, j: (0, 0)),
            pl.BlockSpec((1, d, tn), lambda i, j: (i, 0, j)),
            pl.BlockSpec((1, 1, tn), lambda i, j: (i, 0, j)),
        ],
        out_specs=pl.BlockSpec((1, MOD_ROWS, tn), lambda i, j: (i, 0, j)),
        out_shape=jax.ShapeDtypeStruct((depth, MOD_ROWS, nd), F32),
        compiler_params=_cparams("arbitrary", "arbitrary"),
        name="modulation",
    )(cvecs, mod_w, mod_b.reshape(depth, 1, nd))


def _mod_spec(per_sample, d):
    if per_sample:
        return pl.BlockSpec((1, 1, d), lambda b, t: (b, 0, 0))
    return pl.BlockSpec((1, 1, d), lambda b, t: (0, 0, 0))


def _ab_in_kernel(x_ref, g_ref, sh_ref, sc_ref, w_ref, cs_ref, a_ref, y_ref, *, conv_ch, group_ch):
    u = _norm_modulate(x_ref[0], g_ref[...], sh_ref[0], sc_ref[0]).astype(BF16)
    z = _dot(u, w_ref[...])
    a_ref[0] = z[:, :conv_ch] * _sigmoid(z[:, conv_ch:2 * conv_ch])
    uf = z[:, 2 * conv_ch:].astype(BF16)
    for g in range(FOURIER_GROUPS):
        yg = _dot(uf[:, g * group_ch:(g + 1) * group_ch], cs_ref[...])
        y_ref[0, 0, :, g * group_ch:(g + 1) * group_ch] = yg[:, :group_ch].astype(BF16)
        y_ref[0, 1, :, g * group_ch:(g + 1) * group_ch] = yg[:, group_ch:].astype(BF16)


def _ab_in(x, g, sh, sc, per_sample, w_in, cs, conv_ch, tl):
    n, L, d = x.shape
    ab_in = w_in.shape[1]
    four_ch = ab_in - 2 * conv_ch
    group_ch = four_ch // FOURIER_GROUPS
    kern = functools.partial(_ab_in_kernel, conv_ch=conv_ch, group_ch=group_ch)
    return pl.pallas_call(
        kern,
        grid=(n, L // tl),
        in_specs=[
            pl.BlockSpec((1, tl, d), lambda b, t: (b, t, 0)),
            pl.BlockSpec((1, d), lambda b, t: (0, 0)),
            _mod_spec(per_sample, d),
            _mod_spec(per_sample, d),
            pl.BlockSpec((d, ab_in), lambda b, t: (0, 0)),
            pl.BlockSpec(cs.shape, lambda b, t: (0, 0)),
        ],
        out_specs=[
            pl.BlockSpec((1, tl, conv_ch), lambda b, t: (b, t, 0)),
            pl.BlockSpec((1, 2, tl, four_ch), lambda b, t: (b, 0, t, 0)),
        ],
        out_shape=[
            jax.ShapeDtypeStruct((n, L, conv_ch), F32),
            jax.ShapeDtypeStruct((n, 2, L, four_ch), BF16),
        ],
        compiler_params=_cparams("arbitrary", "arbitrary"),
        name="ab_in",
    )(x, g, sh, sc, w_in, cs)


def _ab_out_kernel(a_ref, y_ref, d_ref, x_ref, g1_ref, cw_ref, cb_ref, lg_ref, lb_ref, wo_ref,
                   o_ref, apad_ref, act_ref, *, L, tq, conv_ch):
    t = pl.program_id(1)

    @pl.when(t == 0)
    def _():
        apad_ref[0:CONV_HALO, :] = jnp.zeros((CONV_HALO, conv_ch), F32)
        apad_ref[CONV_HALO:CONV_HALO + L, :] = a_ref[0]
        apad_ref[CONV_HALO + L:CONV_HALO + L + CONV_HALO, :] = jnp.zeros((CONV_HALO, conv_ch), F32)

    t0 = pl.multiple_of(t * tq, tq)
    win = tq + 8 * CONV_TAP_GROUPS
    for c in range(conv_ch // LANES):
        cols = slice(c * LANES, (c + 1) * LANES)
        full = apad_ref[pl.ds(t0, win), cols]
        acc = jnp.zeros((tq, LANES), F32) + cb_ref[:, cols]
        for r in range(8):
            shifted = full if r == 0 else pltpu.roll(full, win - r, axis=0)
            for j in range(CONV_TAP_GROUPS):
                m = 8 * j + r
                if m == 0:
                    continue
                acc = acc + shifted[8 * j:8 * j + tq, :] * cw_ref[m:m + 1, cols]
        act_ref[:, cols] = acc
    conv = act_ref[...]
    mu = jnp.mean(conv, axis=-1, keepdims=True)
    cen = conv - mu
    var = jnp.mean(cen * cen, axis=-1, keepdims=True)
    ln = cen * lax.rsqrt(var + NORM_EPS) * lg_ref[...] + lb_ref[...]
    act = (ln * _sigmoid(ln)).astype(BF16)
    four = _dot(d_ref[...], y_ref[0]).astype(BF16)
    mix = _dot(act, wo_ref[0:conv_ch, :]) + _dot(four, wo_ref[conv_ch:, :])
    o_ref[0] = x_ref[0] + g1_ref[0] * mix


def _ab_out(a, ystack, dmat, x, g1, per_sample, conv_w, conv_b, ln_g, ln_b, w_out, tq):
    n, L, d = x.shape
    conv_ch = a.shape[-1]
    four_ch = ystack.shape[-1]
    kern = functools.partial(_ab_out_kernel, L=L, tq=tq, conv_ch=conv_ch)
    vec = lambda b, t: (0, 0)
    return pl.pallas_call(
        kern,
        grid=(n, L // tq),
        in_specs=[
            pl.BlockSpec((1, L, conv_ch), lambda b, t: (b, 0, 0)),
            pl.BlockSpec((1, 2 * L, four_ch), lambda b, t: (b, 0, 0)),
            pl.BlockSpec((tq, 2 * L), lambda b, t: (t, 0)),
            pl.BlockSpec((1, tq, d), lambda b, t: (b, t, 0)),
            _mod_spec(per_sample, d),
            pl.BlockSpec(conv_w.shape, vec),
            pl.BlockSpec((1, conv_ch), vec),
            pl.BlockSpec((1, conv_ch), vec),
            pl.BlockSpec((1, conv_ch), vec),
            pl.BlockSpec(w_out.shape, vec),
        ],
        out_specs=pl.BlockSpec((1, tq, d), lambda b, t: (b, t, 0)),
        out_shape=jax.ShapeDtypeStruct((n, L, d), F32),
        scratch_shapes=[
            pltpu.VMEM((L + 2 * CONV_HALO, conv_ch), F32),
            pltpu.VMEM((tq, conv_ch), F32),
        ],
        compiler_params=_cparams("arbitrary", "arbitrary"),
        name="ab_out",
    )(a, ystack, dmat, x, g1, conv_w, conv_b, ln_g, ln_b, w_out)


def _dft_tables(L, group_ch):
    k = np.arange(L, dtype=np.int64)
    ang = 2.0 * np.pi * ((k[:, None] * k[None, :]) % L).astype(np.float64) / L
    dmat = np.concatenate([np.cos(ang), -np.sin(ang)], axis=1) / np.sqrt(L)
    m = np.arange(group_ch, dtype=np.int64)
    angc = 2.0 * np.pi * ((m[:, None] * m[None, :]) % group_ch).astype(np.float64) / group_ch
    cs = np.concatenate([np.cos(angc), np.sin(angc)], axis=1) / np.sqrt(group_ch)
    return jnp.asarray(dmat.astype(np.float32)).astype(BF16), jnp.asarray(cs.astype(np.float32)).astype(BF16)


def _conv_fourier_layer(x, norm_g, sh, sc, g1, per_sample, w_in, conv_w, conv_b, ln_g, ln_b, w_out, tl, tq):
    n, L, d = x.shape
    conv_ch = conv_w.shape[-1]
    four_ch = w_in.shape[1] - 2 * conv_ch
    dmat, cs = _dft_tables(L, four_ch // FOURIER_GROUPS)
    a, y = _ab_in(x, norm_g.reshape(1, d), sh, sc, per_sample, w_in.astype(BF16), cs, conv_ch, tl)
    cw = jnp.concatenate([jnp.zeros((1, conv_ch), F32), conv_w], axis=0)
    return _ab_out(a, y.reshape(n, 2 * L, four_ch), dmat, x, g1, per_sample, cw,
                   conv_b.reshape(1, conv_ch), ln_g.reshape(1, conv_ch), ln_b.reshape(1, conv_ch),
                   w_out.astype(BF16), tq)


def _rope_tables(L):
    rows = L // GRID_W
    row = np.repeat(np.arange(rows, dtype=np.float32), GRID_W)
    col = np.tile(np.arange(GRID_W, dtype=np.float32), rows)
    nf = QK_ROPE // 4
    inv_freq = (np.float32(ROPE_BASE) ** (-np.arange(nf, dtype=np.float32) / np.float32(nf))).astype(np.float32)
    ang = np.concatenate([row[:, None] * inv_freq, col[:, None] * inv_freq], axis=-1).astype(np.float32)
    cos = np.repeat(np.cos(ang.astype(np.float64)), 2, axis=-1)
    sin = np.repeat(np.sin(ang.astype(np.float64)), 2, axis=-1)
    cos_t = np.zeros((L, HEAD_PAD), np.float32)
    sin_t = np.zeros((L, HEAD_PAD), np.float32)
    cos_t[:, :QK_NOPE] = 1.0
    cos_t[:, QK_NOPE:QK_HEAD] = cos
    sin_t[:, QK_NOPE:QK_HEAD] = sin
    return jnp.asarray(cos_t), jnp.asarray(sin_t)


def _pair_swap(w):
    w2 = w.reshape(w.shape[:-1] + (QK_ROPE // 2, 2))
    return jnp.stack([-w2[..., 1], w2[..., 0]], axis=-1).reshape(w.shape)


def _mla_weights(w_in, w_uq, w_ukv):
    d = w_in.shape[0]
    h = MLA_HEADS
    zeros = lambda *s: jnp.zeros(s, F32)
    w_kr = w_in[:, Q_LORA + KV_LORA:]
    kr_a = jnp.concatenate([zeros(d, QK_NOPE), w_kr, zeros(d, HEAD_PAD - QK_HEAD)], axis=1)
    kr_b = jnp.concatenate([zeros(d, QK_NOPE), _pair_swap(w_kr), zeros(d, HEAD_PAD - QK_HEAD)], axis=1)
    w_in_ext = jnp.concatenate([w_in[:, :Q_LORA + KV_LORA], kr_a, kr_b], axis=1)
    wq = w_uq.reshape(Q_LORA, h, QK_HEAD)
    pad = zeros(Q_LORA, h, HEAD_PAD - QK_HEAD)
    wq_a = jnp.concatenate([wq, pad], axis=-1).reshape(Q_LORA, h * HEAD_PAD)
    wq_b = jnp.concatenate([zeros(Q_LORA, h, QK_NOPE), _pair_swap(wq[..., QK_NOPE:]), pad],
                           axis=-1).reshape(Q_LORA, h * HEAD_PAD)
    wkv = w_ukv.reshape(KV_LORA, h, QK_NOPE + V_HEAD)
    wk = jnp.concatenate([wkv[..., :QK_NOPE], zeros(KV_LORA, h, HEAD_PAD - QK_NOPE)],
                         axis=-1).reshape(KV_LORA, h * HEAD_PAD)
    wv = wkv[..., QK_NOPE:].reshape(KV_LORA, h * V_HEAD)
    return (w_in_ext.astype(BF16), jnp.concatenate([wq_a, wq_b], axis=1).astype(BF16),
            jnp.concatenate([wk, wv], axis=1).astype(BF16))


def _rms(x, g):
    return x * lax.rsqrt(jnp.mean(x * x, axis=-1, keepdims=True) + NORM_EPS) * g


def _mla_proj_kernel(*refs, latent):
    if latent:
        (x_ref, g_ref, sh_ref, sc_ref, win_ref, qg_ref, kvg_ref, wq_ref, wkv_ref, cos_ref, sin_ref,
         q_ref, k_ref, v_ref) = refs
    else:
        (x_ref, g_ref, sh_ref, sc_ref, win_ref, kvg_ref, wkv_ref, k_ref, v_ref) = refs
    hp = MLA_HEADS * HEAD_PAD
    u = _norm_modulate(x_ref[0], g_ref[...], sh_ref[0], sc_ref[0]).astype(BF16)
    z = _dot(u, win_ref[...])
    ckv = _rms(z[:, Q_LORA:Q_LORA + KV_LORA], kvg_ref[...]).astype(BF16)
    kv = _dot(ckv, wkv_ref[...])
    kr = z[:, Q_LORA + KV_LORA:Q_LORA + KV_LORA + HEAD_PAD]
    if latent:
        cos = cos_ref[...]
        sin = sin_ref[...]
        kr = kr * cos + z[:, Q_LORA + KV_LORA + HEAD_PAD:] * sin
        cq = _rms(z[:, :Q_LORA], qg_ref[...]).astype(BF16)
        qq = _dot(cq, wq_ref[...])
        for h in range(MLA_HEADS):
            cols = slice(h * HEAD_PAD, (h + 1) * HEAD_PAD)
            qb = qq[:, hp + h * HEAD_PAD:hp + (h + 1) * HEAD_PAD]
            q_ref[0, h] = ((qq[:, cols] * cos + qb * sin) * SCORE_SCALE).T.astype(BF16)
    for h in range(MLA_HEADS):
        cols = slice(h * HEAD_PAD, (h + 1) * HEAD_PAD)
        k_ref[0, h] = (kv[:, cols] + kr).astype(BF16)
    for hpair in range(MLA_HEADS // 2):
        v_ref[0, hpair] = kv[:, hp + hpair * 2 * V_HEAD:hp + (hpair + 1) * 2 * V_HEAD].T.astype(BF16)


def _mla_proj(x, g, sh, sc, per_sample, w_in_ext, q_g, kv_g, wq, wkv, cos_t, sin_t, latent, tl):
    n, L, d = x.shape
    vec = lambda b, t: (0, 0)
    row = lambda b, t: (b, t, 0)
    in_specs = [
        pl.BlockSpec((1, tl, d), row),
        pl.BlockSpec((1, d), vec),
        _mod_spec(per_sample, d),
        _mod_spec(per_sample, d),
        pl.BlockSpec(w_in_ext.shape, vec),
    ]
    args = [x, g, sh, sc, w_in_ext]
    if latent:
        in_specs += [pl.BlockSpec((1, Q_LORA), vec)]
        args += [q_g]
    in_specs += [pl.BlockSpec((1, KV_LORA), vec)]
    args += [kv_g]
    if latent:
        in_specs += [pl.BlockSpec(wq.shape, vec)]
        args += [wq]
    in_specs += [pl.BlockSpec(wkv.shape, vec)]
    args += [wkv]
    nh, nhp = MLA_HEADS, MLA_HEADS // 2
    head_row = lambda b, t: (b, 0, t, 0)
    head_col = lambda b, t: (b, 0, 0, t)
    out_specs = [pl.BlockSpec((1, nh, tl, HEAD_PAD), head_row), pl.BlockSpec((1, nhp, 2 * V_HEAD, tl), head_col)]
    out_shape = [jax.ShapeDtypeStruct((n, nh, L, HEAD_PAD), BF16),
                 jax.ShapeDtypeStruct((n, nhp, 2 * V_HEAD, L), BF16)]
    if latent:
        in_specs += [pl.BlockSpec((tl, HEAD_PAD), lambda b, t: (t, 0))] * 2
        args += [cos_t, sin_t]
        out_specs = [pl.BlockSpec((1, nh, HEAD_PAD, tl), head_col)] + out_specs
        out_shape = [jax.ShapeDtypeStruct((n, nh, HEAD_PAD, L), BF16)] + out_shape
    return pl.pallas_call(
        functools.partial(_mla_proj_kernel, latent=latent),
        grid=(n, L // tl),
        in_specs=in_specs,
        out_specs=out_specs,
        out_shape=out_shape,
        compiler_params=_cparams("arbitrary", "arbitrary"),
        name="mla_proj_lat" if latent else "mla_proj_ctx",
    )(*args)


def _mla_attn_kernel(q_ref, kc_ref, kl_ref, vc_ref, vl_ref, x_ref, g1_ref, wo_ref, o_ref, oh_ref, s_ref,
                     *, n_ctx, n_lat):
    tq = q_ref.shape[3]
    chunks = [(True, 0, n_ctx, 0)]
    chunks += [(False, k0, KEY_CHUNK, n_ctx + k0) for k0 in range(0, n_lat, KEY_CHUNK)]
    even_rows = lax.broadcasted_iota(jnp.int32, (2 * V_HEAD, tq), 0) < V_HEAD

    def scores(h, par):
        qt = q_ref[0, h]
        mx = jnp.full((1, tq), -jnp.inf, F32)
        for is_ctx, k0, kn, r0 in chunks:
            k_ref = kc_ref if is_ctx else kl_ref
            s = _dot(k_ref[0, h, k0:k0 + kn, :], qt)
            s_ref[par, r0:r0 + kn, :] = s
            mx = jnp.maximum(mx, jnp.max(s, axis=0, keepdims=True))
        return mx

    def values(hpair, par, mx):
        den = jnp.zeros((1, tq), F32)
        acc = jnp.zeros((2 * V_HEAD, tq), F32)
        for is_ctx, k0, kn, r0 in chunks:
            v_ref = vc_ref if is_ctx else vl_ref
            p = jnp.exp2(s_ref[par, r0:r0 + kn, :] - mx)
            den = den + jnp.sum(p, axis=0, keepdims=True)
            acc = acc + _dot(v_ref[0, hpair, :, k0:k0 + kn], p.astype(BF16))
        return acc * (1.0 / den)

    def head_pair(hpair, mx_even, last):
        mx_odd = scores(2 * hpair + 1, 1)
        out_even = values(hpair, 0, mx_even)
        mx_next = mx_even if last else scores(2 * hpair + 2, 0)
        out_odd = values(hpair, 1, mx_odd)
        oh_ref[hpair] = jnp.where(even_rows, out_even, out_odd).T.astype(BF16)
        return mx_next

    n_pairs = MLA_HEADS // 2
    mx0 = lax.fori_loop(0, n_pairs - 1, lambda i, mx: head_pair(i, mx, False), scores(0, 0))
    head_pair(n_pairs - 1, mx0, True)
    heads = jnp.concatenate([oh_ref[i] for i in range(n_pairs)], axis=1)
    o_ref[0] = x_ref[0] + g1_ref[0] * _dot(heads, wo_ref[...])


def _mla_attn(q, k_ctx, k_lat, v_ctx, v_lat, x, g1, w_o, tq):
    n, L, d = x.shape
    n_ctx = k_ctx.shape[2]
    nh, nhp = MLA_HEADS, MLA_HEADS // 2
    samp = lambda b, t: (b, 0, 0, 0)
    row = lambda b, t: (b, t, 0)
    assert L % KEY_CHUNK == 0
    return pl.pallas_call(
        functools.partial(_mla_attn_kernel, n_ctx=n_ctx, n_lat=L),
        grid=(n, L // tq),
        in_specs=[
            pl.BlockSpec((1, nh, HEAD_PAD, tq), lambda b, t: (b, 0, 0, t)),
            pl.BlockSpec((1, nh, n_ctx, HEAD_PAD), samp),
            pl.BlockSpec((1, nh, L, HEAD_PAD), samp),
            pl.BlockSpec((1, nhp, 2 * V_HEAD, n_ctx), samp),
            pl.BlockSpec((1, nhp, 2 * V_HEAD, L), samp, pipeline_mode=pl.Buffered(1)),
            pl.BlockSpec((1, tq, d), row),
            _mod_spec(True, d),
            pl.BlockSpec(w_o.shape, lambda b, t: (0, 0), pipeline_mode=pl.Buffered(1)),
        ],
        out_specs=pl.BlockSpec((1, tq, d), row),
        out_shape=jax.ShapeDtypeStruct((n, L, d), F32),
        scratch_shapes=[pltpu.VMEM((nhp, tq, 2 * V_HEAD), BF16),
                        pltpu.VMEM((2, n_ctx + L, tq), F32)],
        compiler_params=_cparams("arbitrary", "arbitrary"),
        name="mla_attn",
    )(q, k_ctx, k_lat, v_ctx, v_lat, x, g1, w_o)


def _route_kernel(x_ref, g_ref, sh_ref, sc_ref, wr_ref, u_ref, slot_ref, aff_ref, slott_ref,
                  afft_ref, tri_ref, *, L, tl, cap):
    b = pl.program_id(0)
    t = pl.program_id(1)
    ne = N_EXPERTS

    @pl.when((b == 0) & (t == 0))
    def _():
        r = lax.broadcasted_iota(jnp.int32, (L, L), 0)
        c = lax.broadcasted_iota(jnp.int32, (L, L), 1)
        tri_ref[...] = jnp.where(r < c, 1.0, 0.0).astype(BF16)

    u = _norm_modulate(x_ref[0], g_ref[...], sh_ref[0], sc_ref[0]).astype(BF16)
    u_ref[0] = u
    logits = _dot_nt(wr_ref[...], u)
    e = jnp.exp(logits - jnp.max(logits, axis=0, keepdims=True))
    t0 = pl.multiple_of(t * tl, tl)
    afft_ref[:, pl.ds(t0, tl)] = e / jnp.sum(e, axis=0, keepdims=True)

    @pl.when(t == pl.num_programs(1) - 1)
    def _():
        aff = afft_ref[...]

        def as_f32(bits):
            return lax.bitcast_convert_type(bits, F32)

        def count_ge(th):
            return jnp.sum(jnp.where(aff >= th, 1.0, 0.0), axis=1, keepdims=True)

        def body(_, carry):
            lo, hi = carry
            mid = lo + ((hi - lo) >> 1)
            ok = count_ge(as_f32(mid)) >= cap
            return jnp.where(ok, mid, lo), jnp.where(ok, hi, mid)

        lo0 = jnp.zeros((ne, 1), jnp.int32)
        hi0 = jnp.full((ne, 1), 0x7F800000, jnp.int32)
        lo_bits, hi_bits = lax.fori_loop(0, 31, body, (lo0, hi0))
        gt = aff >= as_f32(hi_bits)
        eq = (aff >= as_f32(lo_bits)) & jnp.logical_not(gt)
        need = cap - jnp.sum(jnp.where(gt, 1.0, 0.0), axis=1, keepdims=True)
        rank_eq = _dot(jnp.where(eq, 1.0, 0.0).astype(BF16), tri_ref[...])
        sel = gt | (eq & (rank_eq < need))
        pos = _dot(jnp.where(sel, 1.0, 0.0).astype(BF16), tri_ref[...])
        slot = jnp.where(sel, pos, -1.0)
        slot_i = slot.astype(jnp.int32)
        for ex in range(ne):
            slot_ref[0, ex] = slot_i[ex:ex + 1, :]
            aff_ref[0, ex] = aff[ex:ex + 1, :]
        padded = jnp.concatenate([slot, jnp.full((LANES - ne, L), -1.0, F32)], axis=0)
        slott_ref[0] = padded.T


def _route(x, g, sh, sc, per_sample, w_router_t, cap, tl):
    n, L, d = x.shape
    ne = N_EXPERTS
    vec = lambda b, t: (0, 0)
    samp4 = lambda b, t: (b, 0, 0, 0)
    return pl.pallas_call(
        functools.partial(_route_kernel, L=L, tl=tl, cap=cap),
        grid=(n, L // tl),
        in_specs=[
            pl.BlockSpec((1, tl, d), lambda b, t: (b, t, 0)),
            pl.BlockSpec((1, d), vec),
            _mod_spec(per_sample, d),
            _mod_spec(per_sample, d),
            pl.BlockSpec((ne, d), vec),
        ],
        out_specs=[
            pl.BlockSpec((1, tl, d), lambda b, t: (b, t, 0)),
            pl.BlockSpec((1, ne, 1, L), samp4),
            pl.BlockSpec((1, ne, 1, L), samp4),
            pl.BlockSpec((1, L, LANES), lambda b, t: (b, 0, 0)),
        ],
        out_shape=[
            jax.ShapeDtypeStruct((n, L, d), BF16),
            jax.ShapeDtypeStruct((n, ne, 1, L), jnp.int32),
            jax.ShapeDtypeStruct((n, ne, 1, L), F32),
            jax.ShapeDtypeStruct((n, L, LANES), F32),
        ],
        scratch_shapes=[pltpu.VMEM((ne, L), F32), pltpu.VMEM((L, L), BF16)],
        compiler_params=_cparams("arbitrary", "arbitrary"),
        name="route",
    )(x, g, sh, sc, w_router_t)


def _expert_kernel(u_ref, slot_ref, aff_ref, w1_ref, w3_ref, w2_ref, y_ref, w1b, w3b, w2b, *, nb, cap):
    @pl.when(pl.program_id(1) == 0)
    def _():
        w1b[...] = w1_ref[0, 0].astype(BF16)
        w3b[...] = w3_ref[0, 0].astype(BF16)
        w2b[...] = w2_ref[0, 0].astype(BF16)

    L = u_ref.shape[1]
    slot_iota = lax.broadcasted_iota(jnp.int32, (cap, L), 0)
    xs, gates = [], []
    for s in range(nb):
        hit = slot_ref[s, 0] == slot_iota
        xs.append(_dot(jnp.where(hit, 1.0, 0.0).astype(BF16), u_ref[s]).astype(BF16))
        gates.append(jnp.sum(jnp.where(hit, aff_ref[s, 0], 0.0), axis=1, keepdims=True))
    xg = xs[0] if nb == 1 else jnp.concatenate(xs, axis=0)
    gate = gates[0] if nb == 1 else jnp.concatenate(gates, axis=0)
    h1 = _dot(xg, w1b[...])
    hid = (h1 * _sigmoid(h1) * _dot(xg, w3b[...])).astype(BF16)
    y = (_dot(hid, w2b[...]) * gate).astype(BF16)
    for s in range(nb):
        y_ref[0, s] = y[s * cap:(s + 1) * cap]


def _experts(u, slot, aff, layer, w1, w3, w2, cap, nb):
    n, L, d = u.shape
    _, ne, _, ff = w1.shape
    wspec = lambda shape: pl.BlockSpec((1, 1) + shape, lambda e, b: (layer, e, 0, 0))
    return pl.pallas_call(
        functools.partial(_expert_kernel, nb=nb, cap=cap),
        grid=(ne, n // nb),
        in_specs=[
            pl.BlockSpec((nb, L, d), lambda e, b: (b, 0, 0)),
            pl.BlockSpec((nb, 1, 1, L), lambda e, b: (b, e, 0, 0)),
            pl.BlockSpec((nb, 1, 1, L), lambda e, b: (b, e, 0, 0)),
            wspec((d, ff)), wspec((d, ff)), wspec((ff, d)),
        ],
        out_specs=pl.BlockSpec((1, nb, cap, d), lambda e, b: (e, b, 0, 0)),
        out_shape=jax.ShapeDtypeStruct((ne, n, cap, d), BF16),
        scratch_shapes=[pltpu.VMEM((d, ff), BF16), pltpu.VMEM((d, ff), BF16), pltpu.VMEM((ff, d), BF16)],
        compiler_params=_cparams("arbitrary", "arbitrary"),
        name="experts",
    )(u, slot, aff, w1, w3, w2)


def _combine_kernel(*refs, cap, final):
    if final:
        y_ref, st_ref, x_ref, g2_ref, fg_ref, o_ref = refs
    else:
        y_ref, st_ref, x_ref, g2_ref, o_ref = refs
    tl = x_ref.shape[1]
    d = x_ref.shape[2]
    st = st_ref[0]
    lane = lax.broadcasted_iota(jnp.int32, (tl, cap), 1).astype(F32)
    onehot = jnp.concatenate(
        [jnp.where(st[:, e:e + 1] == lane, 1.0, 0.0).astype(BF16) for e in range(N_EXPERTS)], axis=1)
    moe = _dot(onehot, y_ref[...].reshape(N_EXPERTS * cap, d))
    out = x_ref[0] + g2_ref[0] * moe
    if final:
        out = _rms(out, fg_ref[...])
    o_ref[0] = out


def _combine(y, slot_t, x, g2, per_sample, cap, tl, final_g=None):
    n, L, d = x.shape
    ne = N_EXPERTS
    final = final_g is not None
    in_specs = [
        pl.BlockSpec((ne, 1, cap, d), lambda b, t: (0, b, 0, 0)),
        pl.BlockSpec((1, tl, LANES), lambda b, t: (b, t, 0)),
        pl.BlockSpec((1, tl, d), lambda b, t: (b, t, 0)),
        _mod_spec(per_sample, d),
    ]
    args = [y, slot_t, x, g2]
    if final:
        in_specs += [pl.BlockSpec((1, d), lambda b, t: (0, 0))]
        args += [final_g.reshape(1, d)]
    return pl.pallas_call(
        functools.partial(_combine_kernel, cap=cap, final=final),
        grid=(n, L // tl),
        in_specs=in_specs,
        out_specs=pl.BlockSpec((1, tl, d), lambda b, t: (b, t, 0)),
        out_shape=jax.ShapeDtypeStruct((n, L, d), F32),
        compiler_params=_cparams("arbitrary", "arbitrary"),
        name="combine",
    )(*args)


def _ec_moe_layer(x, norm_g, sh, sc, g2, per_sample, layer, w_router, w1, w3, w2, tl, nb, final_g=None):
    n, L, d = x.shape
    cap = EC_CAPACITY_FACTOR * L // N_EXPERTS
    u, slot, aff, slot_t = _route(x, norm_g.reshape(1, d), sh, sc, per_sample,
                                  w_router.T.astype(BF16), cap, tl)
    y = _experts(u, slot, aff, layer, w1, w3, w2, cap, nb)
    return _combine(y, slot_t, x, g2, per_sample, cap, tl, final_g)


def kernel(x, c, ctx, c_ctx, mod_w, mod_b, norm1_g, norm2_g, ab_w_in, ab_conv_w, ab_conv_b, ab_ln_g, ab_ln_b, ab_w_out, mla_w_in, mla_q_norm_g, mla_kv_norm_g, mla_w_uq, mla_w_ukv, mla_w_o, moe_w_router, moe_w1, moe_w3, moe_w2, final_g):
    n, L, d = x.shape
    n_ctx = ctx.shape[1]
    depth = mod_w.shape[0]
    assert n < MOD_ROWS
    cvecs = jnp.concatenate([c, c_ctx[None, :], jnp.zeros((MOD_ROWS - n - 1, d), F32)], axis=0)
    mod = _modulation(cvecs, mod_w, mod_b)

    def mods(i, ctx_stream):
        rows = mod[i, n:n + 1] if ctx_stream else mod[i, :n]
        return [rows[:, k * d:(k + 1) * d].reshape(-1, 1, d) for k in range(N_MOD)]

    x_lat, x_ctx = x, ctx
    for i in range(depth):
        ctx_out = i < depth - 1
        last = i == depth - 1
        j = i // 2
        sh1, sc1, g1, sh2, sc2, g2 = mods(i, False)
        csh1, csc1, cg1, csh2, csc2, cg2 = mods(i, True)
        if i % 2 == 0:
            ab = (ab_w_in[j], ab_conv_w[j], ab_conv_b[j], ab_ln_g[j], ab_ln_b[j], ab_w_out[j])
            x_lat = _conv_fourier_layer(x_lat, norm1_g[i], sh1, sc1, g1, True, *ab, tl=512, tq=256)
            if ctx_out:
                x_ctx = _conv_fourier_layer(x_ctx, norm1_g[i], csh1, csc1, cg1, False, *ab,
                                            tl=n_ctx, tq=n_ctx)
        else:
            assert not ctx_out, "context-stream attention output is only needed for deeper stacks"
            w_in_ext, wq, wkv = _mla_weights(mla_w_in[j], mla_w_uq[j], mla_w_ukv[j])
            cos_t, sin_t = _rope_tables(L)
            g = norm1_g[i].reshape(1, d)
            q_g = mla_q_norm_g[j].reshape(1, Q_LORA)
            kv_g = mla_kv_norm_g[j].reshape(1, KV_LORA)
            q, k_lat, v_lat = _mla_proj(x_lat, g, sh1, sc1, True, w_in_ext, q_g, kv_g, wq, wkv,
                                        cos_t, sin_t, True, 512)
            k_ctx, v_ctx = _mla_proj(x_ctx, g, csh1, csc1, False,
                                     w_in_ext[:, :Q_LORA + KV_LORA + HEAD_PAD], None, kv_g, None, wkv,
                                     None, None, False, n_ctx)
            x_lat = _mla_attn(q, k_ctx, k_lat, v_ctx, v_lat, x_lat, g1, mla_w_o[j].astype(BF16), 512)
        moe = (i, moe_w_router[i], moe_w1, moe_w3, moe_w2)
        x_lat = _ec_moe_layer(x_lat, norm2_g[i], sh2, sc2, g2, True, *moe, tl=512, nb=1,
                              final_g=final_g if last else None)
        if ctx_out:
            x_ctx = _ec_moe_layer(x_ctx, norm2_g[i], csh2, csc2, cg2, False, *moe, tl=n_ctx, nb=n)
    return x_lat
```

```python
import functools

import numpy as np
import jax
import jax.numpy as jnp
from jax import lax
from jax.experimental import pallas as pl
from jax.experimental.pallas import tpu as pltpu

F32 = jnp.float32
BF16 = jnp.bfloat16

NORM_EPS = 1e-6
N_MOD = 6
GRID_W = 64
CONV_WIDTH = 31
CONV_HALO = 16
CONV_TAP_GROUPS = 4
CONV_ROWS = 64
FOURIER_GROUPS = 4
MLA_HEADS = 16
QK_NOPE = 64
QK_ROPE = 32
V_HEAD = 64
V_ROWS = 2 * V_HEAD
QK_HEAD = QK_NOPE + QK_ROPE
HEAD_PAD = 128
Q_LORA = 256
KV_LORA = 128
ROPE_BASE = 10000.0
SCORE_SCALE = float(QK_HEAD ** -0.5 * np.log2(np.e))
KEY_CHUNK = 512
N_EXPERTS = 16
PREFIX_BLOCK = 256
EC_CAPACITY_FACTOR = 2
MOD_ROWS = 24
LANES = 128
VMEM_LIMIT = 56 * 1024 * 1024


def _cparams(*sem):
    return pltpu.CompilerParams(dimension_semantics=sem, vmem_limit_bytes=VMEM_LIMIT)


def _dot(a, b):
    return jnp.dot(a, b, preferred_element_type=F32)


def _dot_nt(a, b):
    return lax.dot_general(a, b, (((1,), (1,)), ((), ())), preferred_element_type=F32)


def _sigmoid(x):
    return 1.0 / (1.0 + jnp.exp(-x))


def _norm_modulate(x, g, shift, scale):
    ms = jnp.mean(x * x, axis=-1, keepdims=True)
    y = x * lax.rsqrt(ms + NORM_EPS) * g
    return y * (1.0 + scale) + shift


def _mod_kernel(c_ref, w_ref, b_ref, o_ref):
    c = c_ref[...]
    s = (c * _sigmoid(c)).astype(BF16)
    o_ref[0] = _dot(s, w_ref[0].astype(BF16)) + b_ref[0]


def _modulation(cvecs, mod_w, mod_b):
    depth, d, nd = mod_w.shape
    tn = 1024
    return pl.pallas_call(
        _mod_kernel,
        grid=(depth, nd // tn),
        in_specs=[
            pl.BlockSpec((MOD_ROWS, d), lambda i, j: (0, 0)),
            pl.BlockSpec((1, d, tn), lambda i, j: (i, 0, j)),
            pl.BlockSpec((1, 1, tn), lambda i, j: (i, 0, j)),
        ],
        out_specs=pl.BlockSpec((1, MOD_ROWS, tn), lambda i, j: (i, 0, j)),
        out_shape=jax.ShapeDtypeStruct((depth, MOD_ROWS, nd), F32),
        compiler_params=_cparams("arbitrary", "arbitrary"),
        name="modulation",
    )(cvecs, mod_w, mod_b.reshape(depth, 1, nd))


def _mod_spec(per_sample, d):
    if per_sample:
        return pl.BlockSpec((1, 1, d), lambda b, t: (b, 0, 0))
    return pl.BlockSpec((1, 1, d), lambda b, t: (0, 0, 0))


def _ab_in_kernel(x_ref, g_ref, sh_ref, sc_ref, w_ref, cs_ref, a_ref, y_ref, *, conv_ch, group_ch):
    u = _norm_modulate(x_ref[0], g_ref[...], sh_ref[0], sc_ref[0]).astype(BF16)
    z = _dot(u, w_ref[...])
    a_ref[0] = z[:, :conv_ch] * _sigmoid(z[:, conv_ch:2 * conv_ch])
    uf = z[:, 2 * conv_ch:].astype(BF16)
    for g in range(FOURIER_GROUPS):
        yg = _dot(uf[:, g * group_ch:(g + 1) * group_ch], cs_ref[...])
        y_ref[0, 0, :, g * group_ch:(g + 1) * group_ch] = yg[:, :group_ch].astype(BF16)
        y_ref[0, 1, :, g * group_ch:(g + 1) * group_ch] = yg[:, group_ch:].astype(BF16)


def _ab_in(x, g, sh, sc, per_sample, w_in, cs, conv_ch, tl):
    n, L, d = x.shape
    ab_in = w_in.shape[1]
    four_ch = ab_in - 2 * conv_ch
    group_ch = four_ch // FOURIER_GROUPS
    kern = functools.partial(_ab_in_kernel, conv_ch=conv_ch, group_ch=group_ch)
    return pl.pallas_call(
        kern,
        grid=(n, L // tl),
        in_specs=[
            pl.BlockSpec((1, tl, d), lambda b, t: (b, t, 0)),
            pl.BlockSpec((1, d), lambda b, t: (0, 0)),
            _mod_spec(per_sample, d),
            _mod_spec(per_sample, d),
            pl.BlockSpec((d, ab_in), lambda b, t: (0, 0)),
            pl.BlockSpec(cs.shape, lambda b, t: (0, 0)),
        ],
        out_specs=[
            pl.BlockSpec((1, tl, conv_ch), lambda b, t: (b, t, 0)),
            pl.BlockSpec((1, 2, tl, four_ch), lambda b, t: (b, 0, t, 0)),
        ],
        out_shape=[
            jax.ShapeDtypeStruct((n, L, conv_ch), F32),
            jax.ShapeDtypeStruct((n, 2, L, four_ch), BF16),
        ],
        compiler_params=_cparams("arbitrary", "arbitrary"),
        name="ab_in",
    )(x, g, sh, sc, w_in, cs)


def _ab_out_kernel(a_ref, y_ref, d_ref, x_ref, g1_ref, cw_ref, cb_ref, lg_ref, lb_ref, wo_ref,
                   o_ref, apad_ref, act_ref, four_ref, *, L, tq, conv_ch):
    t = pl.program_id(1)

    @pl.when(t == 0)
    def _():
        apad_ref[0:CONV_HALO, :] = jnp.zeros((CONV_HALO, conv_ch), F32)
        apad_ref[CONV_HALO:CONV_HALO + L, :] = a_ref[0]
        apad_ref[CONV_HALO + L:CONV_HALO + L + CONV_HALO, :] = jnp.zeros((CONV_HALO, conv_ch), F32)

    win = CONV_ROWS + 8 * CONV_TAP_GROUPS
    n_rb = tq // CONV_ROWS
    kc = d_ref.shape[3]

    def conv_and_dft(rb, carry):
        four_ref[rb] = _dot(d_ref[0, rb], y_ref[0, pl.ds(pl.multiple_of(rb * kc, kc), kc), :])
        r0 = pl.multiple_of(t * tq + rb * CONV_ROWS, CONV_ROWS)
        for c in range(conv_ch // LANES):
            cols = slice(c * LANES, (c + 1) * LANES)
            full = apad_ref[pl.ds(r0, win), cols]
            acc = jnp.zeros((CONV_ROWS, LANES), F32) + cb_ref[:, cols]
            for r in range(8):
                shifted = full if r == 0 else pltpu.roll(full, win - r, axis=0)
                for j in range(CONV_TAP_GROUPS):
                    m = 8 * j + r
                    if m == 0:
                        continue
                    acc = acc + shifted[8 * j:8 * j + CONV_ROWS, :] * cw_ref[m:m + 1, cols]
            act_ref[pl.ds(pl.multiple_of(rb * CONV_ROWS, CONV_ROWS), CONV_ROWS), cols] = acc
        return carry

    lax.fori_loop(0, n_rb, conv_and_dft, 0)
    four = four_ref[0]
    for rb in range(1, n_rb):
        four = four + four_ref[rb]
    conv = act_ref[...]
    mu = jnp.mean(conv, axis=-1, keepdims=True)
    cen = conv - mu
    var = jnp.mean(cen * cen, axis=-1, keepdims=True)
    ln = cen * lax.rsqrt(var + NORM_EPS) * lg_ref[...] + lb_ref[...]
    act = (ln * _sigmoid(ln)).astype(BF16)
    mix = _dot(act, wo_ref[0:conv_ch, :]) + _dot(four.astype(BF16), wo_ref[conv_ch:, :])
    o_ref[0] = x_ref[0] + g1_ref[0] * mix


def _ab_out(a, ystack, dmat, x, g1, per_sample, conv_w, conv_b, ln_g, ln_b, w_out, tq):
    n, L, d = x.shape
    conv_ch = a.shape[-1]
    four_ch = ystack.shape[-1]
    kern = functools.partial(_ab_out_kernel, L=L, tq=tq, conv_ch=conv_ch)
    vec = lambda b, t: (0, 0)
    return pl.pallas_call(
        kern,
        grid=(n, L // tq),
        in_specs=[
            pl.BlockSpec((1, L, conv_ch), lambda b, t: (b, 0, 0)),
            pl.BlockSpec((1, 2 * L, four_ch), lambda b, t: (b, 0, 0)),
            pl.BlockSpec((1,) + dmat.shape[1:], lambda b, t: (t, 0, 0, 0)),
            pl.BlockSpec((1, tq, d), lambda b, t: (b, t, 0)),
            _mod_spec(per_sample, d),
            pl.BlockSpec(conv_w.shape, vec),
            pl.BlockSpec((1, conv_ch), vec),
            pl.BlockSpec((1, conv_ch), vec),
            pl.BlockSpec((1, conv_ch), vec),
            pl.BlockSpec(w_out.shape, vec),
        ],
        out_specs=pl.BlockSpec((1, tq, d), lambda b, t: (b, t, 0)),
        out_shape=jax.ShapeDtypeStruct((n, L, d), F32),
        scratch_shapes=[
            pltpu.VMEM((L + 2 * CONV_HALO, conv_ch), F32),
            pltpu.VMEM((tq, conv_ch), F32),
            pltpu.VMEM((tq // CONV_ROWS, tq, four_ch), F32),
        ],
        compiler_params=_cparams("arbitrary", "arbitrary"),
        name="ab_out",
    )(a, ystack, dmat, x, g1, conv_w, conv_b, ln_g, ln_b, w_out)


def _dft_tables(L, group_ch, tq):
    k = np.arange(L, dtype=np.int64)
    ang = 2.0 * np.pi * ((k[:, None] * k[None, :]) % L).astype(np.float64) / L
    dmat = np.concatenate([np.cos(ang), -np.sin(ang)], axis=1) / np.sqrt(L)
    n_rb = tq // CONV_ROWS
    dmat = dmat.reshape(L // tq, tq, n_rb, 2 * L // n_rb).transpose(0, 2, 1, 3)
    m = np.arange(group_ch, dtype=np.int64)
    angc = 2.0 * np.pi * ((m[:, None] * m[None, :]) % group_ch).astype(np.float64) / group_ch
    cs = np.concatenate([np.cos(angc), np.sin(angc)], axis=1) / np.sqrt(group_ch)
    return jnp.asarray(dmat.astype(np.float32)).astype(BF16), jnp.asarray(cs.astype(np.float32)).astype(BF16)


def _conv_fourier_layer(x, norm_g, sh, sc, g1, per_sample, w_in, conv_w, conv_b, ln_g, ln_b, w_out, tl, tq):
    n, L, d = x.shape
    conv_ch = conv_w.shape[-1]
    four_ch = w_in.shape[1] - 2 * conv_ch
    dmat, cs = _dft_tables(L, four_ch // FOURIER_GROUPS, tq)
    a, y = _ab_in(x, norm_g.reshape(1, d), sh, sc, per_sample, w_in.astype(BF16), cs, conv_ch, tl)
    cw = jnp.concatenate([jnp.zeros((1, conv_ch), F32), conv_w], axis=0)
    return _ab_out(a, y.reshape(n, 2 * L, four_ch), dmat, x, g1, per_sample, cw,
                   conv_b.reshape(1, conv_ch), ln_g.reshape(1, conv_ch), ln_b.reshape(1, conv_ch),
                   w_out.astype(BF16), tq)


def _rope_tables(L):
    rows = L // GRID_W
    row = np.repeat(np.arange(rows, dtype=np.float32), GRID_W)
    col = np.tile(np.arange(GRID_W, dtype=np.float32), rows)
    nf = QK_ROPE // 4
    inv_freq = (np.float32(ROPE_BASE) ** (-np.arange(nf, dtype=np.float32) / np.float32(nf))).astype(np.float32)
    ang = np.concatenate([row[:, None] * inv_freq, col[:, None] * inv_freq], axis=-1).astype(np.float32)
    cos = np.repeat(np.cos(ang.astype(np.float64)), 2, axis=-1)
    sin = np.repeat(np.sin(ang.astype(np.float64)), 2, axis=-1)
    cos_t = np.zeros((L, HEAD_PAD), np.float32)
    sin_t = np.zeros((L, HEAD_PAD), np.float32)
    cos_t[:, :QK_NOPE] = 1.0
    cos_t[:, QK_NOPE:QK_HEAD] = cos
    sin_t[:, QK_NOPE:QK_HEAD] = sin
    return jnp.asarray(cos_t), jnp.asarray(sin_t)


def _pair_swap(w):
    w2 = w.reshape(w.shape[:-1] + (QK_ROPE // 2, 2))
    return jnp.stack([-w2[..., 1], w2[..., 0]], axis=-1).reshape(w.shape)


def _mla_weights(w_in, w_uq, w_ukv):
    d = w_in.shape[0]
    h = MLA_HEADS
    zeros = lambda *s: jnp.zeros(s, F32)
    w_kr = w_in[:, Q_LORA + KV_LORA:]
    kr_a = jnp.concatenate([zeros(d, QK_NOPE), w_kr, zeros(d, HEAD_PAD - QK_HEAD)], axis=1)
    kr_b = jnp.concatenate([zeros(d, QK_NOPE), _pair_swap(w_kr), zeros(d, HEAD_PAD - QK_HEAD)], axis=1)
    w_in_ext = jnp.concatenate([w_in[:, :Q_LORA + KV_LORA], kr_a, kr_b], axis=1)
    wq = w_uq.reshape(Q_LORA, h, QK_HEAD)
    pad = zeros(Q_LORA, h, HEAD_PAD - QK_HEAD)
    wq_a = jnp.concatenate([wq, pad], axis=-1).reshape(Q_LORA, h * HEAD_PAD)
    wq_b = jnp.concatenate([zeros(Q_LORA, h, QK_NOPE), _pair_swap(wq[..., QK_NOPE:]), pad],
                           axis=-1).reshape(Q_LORA, h * HEAD_PAD)
    wkv = w_ukv.reshape(KV_LORA, h, QK_NOPE + V_HEAD)
    wk = jnp.concatenate([wkv[..., :QK_NOPE], zeros(KV_LORA, h, HEAD_PAD - QK_NOPE)],
                         axis=-1).reshape(KV_LORA, h * HEAD_PAD)
    wv = wkv[..., QK_NOPE:].reshape(KV_LORA, h * V_HEAD)
    return (w_in_ext.astype(BF16), jnp.concatenate([wq_a, wq_b], axis=1).astype(BF16),
            jnp.concatenate([wk, wv], axis=1).astype(BF16))


def _rms(x, g):
    return x * lax.rsqrt(jnp.mean(x * x, axis=-1, keepdims=True) + NORM_EPS) * g


def _mla_proj_kernel(*refs, latent):
    if latent:
        (x_ref, g_ref, sh_ref, sc_ref, win_ref, qg_ref, kvg_ref, wq_ref, wkv_ref, cos_ref, sin_ref,
         q_ref, k_ref, v_ref) = refs
    else:
        (x_ref, g_ref, sh_ref, sc_ref, win_ref, kvg_ref, wkv_ref, k_ref, v_ref) = refs
    hp = MLA_HEADS * HEAD_PAD
    u = _norm_modulate(x_ref[0], g_ref[...], sh_ref[0], sc_ref[0]).astype(BF16)
    z = _dot(u, win_ref[...])
    ckv = _rms(z[:, Q_LORA:Q_LORA + KV_LORA], kvg_ref[...]).astype(BF16)
    kv = _dot(ckv, wkv_ref[...])
    kr = z[:, Q_LORA + KV_LORA:Q_LORA + KV_LORA + HEAD_PAD]
    if latent:
        cos = cos_ref[...]
        sin = sin_ref[...]
        kr = kr * cos + z[:, Q_LORA + KV_LORA + HEAD_PAD:] * sin
        cq = _rms(z[:, :Q_LORA], qg_ref[...]).astype(BF16)
        qq = _dot(cq, wq_ref[...])
        for h in range(MLA_HEADS):
            cols = slice(h * HEAD_PAD, (h + 1) * HEAD_PAD)
            qb = qq[:, hp + h * HEAD_PAD:hp + (h + 1) * HEAD_PAD]
            q_ref[0, h] = ((qq[:, cols] * cos + qb * sin) * SCORE_SCALE).T.astype(BF16)
    for h in range(MLA_HEADS):
        cols = slice(h * HEAD_PAD, (h + 1) * HEAD_PAD)
        k_ref[0, h] = (kv[:, cols] + kr).astype(BF16)
    for hpair in range(MLA_HEADS // 2):
        v_ref[0, hpair] = kv[:, hp + hpair * 2 * V_HEAD:hp + (hpair + 1) * 2 * V_HEAD].T.astype(BF16)


def _mla_proj(x, g, sh, sc, per_sample, w_in_ext, q_g, kv_g, wq, wkv, cos_t, sin_t, latent, tl):
    n, L, d = x.shape
    vec = lambda b, t: (0, 0)
    row = lambda b, t: (b, t, 0)
    in_specs = [
        pl.BlockSpec((1, tl, d), row),
        pl.BlockSpec((1, d), vec),
        _mod_spec(per_sample, d),
        _mod_spec(per_sample, d),
        pl.BlockSpec(w_in_ext.shape, vec),
    ]
    args = [x, g, sh, sc, w_in_ext]
    if latent:
        in_specs += [pl.BlockSpec((1, Q_LORA), vec)]
        args += [q_g]
    in_specs += [pl.BlockSpec((1, KV_LORA), vec)]
    args += [kv_g]
    if latent:
        in_specs += [pl.BlockSpec(wq.shape, vec)]
        args += [wq]
    in_specs += [pl.BlockSpec(wkv.shape, vec)]
    args += [wkv]
    nh, nhp = MLA_HEADS, MLA_HEADS // 2
    head_row = lambda b, t: (b, 0, t, 0)
    head_col = lambda b, t: (b, 0, 0, t)
    out_specs = [pl.BlockSpec((1, nh, tl, HEAD_PAD), head_row), pl.BlockSpec((1, nhp, V_ROWS, tl), head_col)]
    out_shape = [jax.ShapeDtypeStruct((n, nh, L, HEAD_PAD), BF16),
                 jax.ShapeDtypeStruct((n, nhp, V_ROWS, L), BF16)]
    if latent:
        in_specs += [pl.BlockSpec((tl, HEAD_PAD), lambda b, t: (t, 0))] * 2
        args += [cos_t, sin_t]
        out_specs = [pl.BlockSpec((1, nh, HEAD_PAD, tl), head_col)] + out_specs
        out_shape = [jax.ShapeDtypeStruct((n, nh, HEAD_PAD, L), BF16)] + out_shape
    return pl.pallas_call(
        functools.partial(_mla_proj_kernel, latent=latent),
        grid=(n, L // tl),
        in_specs=in_specs,
        out_specs=out_specs,
        out_shape=out_shape,
        compiler_params=_cparams("arbitrary", "arbitrary"),
        name="mla_proj_lat" if latent else "mla_proj_ctx",
    )(*args)


def _mla_attn_kernel(q_ref, kc_ref, kl_ref, vc_ref, vl_ref, x_ref, g1_ref, wo_ref, o_ref, oh_ref, s_ref,
                     *, n_ctx, n_lat):
    tq = q_ref.shape[3]
    chunks = [(True, 0, n_ctx, 0)]
    chunks += [(False, k0, KEY_CHUNK, n_ctx + k0) for k0 in range(0, n_lat, KEY_CHUNK)]
    even_rows = lax.broadcasted_iota(jnp.int32, (2 * V_HEAD, tq), 0) < V_HEAD

    def scores(h, par):
        qt = q_ref[0, h]
        mx = jnp.full((1, tq), -jnp.inf, F32)
        for is_ctx, k0, kn, r0 in chunks:
            k_ref = kc_ref if is_ctx else kl_ref
            s = _dot(k_ref[0, h, k0:k0 + kn, :], qt)
            s_ref[par, r0:r0 + kn, :] = s
            mx = jnp.maximum(mx, jnp.max(s, axis=0, keepdims=True))
        return mx

    def values(hpair, par, mx):
        den = jnp.zeros((1, tq), F32)
        acc = jnp.zeros((V_ROWS, tq), F32)
        for is_ctx, k0, kn, r0 in chunks:
            v_ref = vc_ref if is_ctx else vl_ref
            p = jnp.exp2(s_ref[par, r0:r0 + kn, :] - mx)
            den = den + jnp.sum(p, axis=0, keepdims=True)
            acc = acc + _dot(v_ref[0, hpair, :, k0:k0 + kn], p.astype(BF16))
        return acc * (1.0 / den)

    def head_pair(hpair, mx_even, last):
        mx_odd = scores(2 * hpair + 1, 1)
        out_even = values(hpair, 0, mx_even)
        mx_next = mx_even if last else scores(2 * hpair + 2, 0)
        out_odd = values(hpair, 1, mx_odd)
        oh_ref[hpair] = jnp.where(even_rows, out_even, out_odd).T.astype(BF16)
        return mx_next

    n_pairs = MLA_HEADS // 2
    mx0 = lax.fori_loop(0, n_pairs - 1, lambda i, mx: head_pair(i, mx, False), scores(0, 0))
    head_pair(n_pairs - 1, mx0, True)
    heads = jnp.concatenate([oh_ref[i] for i in range(n_pairs)], axis=1)
    o_ref[0] = x_ref[0] + g1_ref[0] * _dot(heads, wo_ref[...])


def _mla_attn(q, k_ctx, k_lat, v_ctx, v_lat, x, g1, w_o, tq):
    n, L, d = x.shape
    n_ctx = k_ctx.shape[2]
    nh, nhp = MLA_HEADS, MLA_HEADS // 2
    samp = lambda b, t: (b, 0, 0, 0)
    row = lambda b, t: (b, t, 0)
    assert L % KEY_CHUNK == 0
    return pl.pallas_call(
        functools.partial(_mla_attn_kernel, n_ctx=n_ctx, n_lat=L),
        grid=(n, L // tq),
        in_specs=[
            pl.BlockSpec((1, nh, HEAD_PAD, tq), lambda b, t: (b, 0, 0, t)),
            pl.BlockSpec((1, nh, n_ctx, HEAD_PAD), samp),
            pl.BlockSpec((1, nh, L, HEAD_PAD), samp),
            pl.BlockSpec((1, nhp, V_ROWS, n_ctx), samp),
            pl.BlockSpec((1, nhp, V_ROWS, L), samp, pipeline_mode=pl.Buffered(1)),
            pl.BlockSpec((1, tq, d), row),
            _mod_spec(True, d),
            pl.BlockSpec(w_o.shape, lambda b, t: (0, 0), pipeline_mode=pl.Buffered(1)),
        ],
        out_specs=pl.BlockSpec((1, tq, d), row),
        out_shape=jax.ShapeDtypeStruct((n, L, d), F32),
        scratch_shapes=[pltpu.VMEM((nhp, tq, 2 * V_HEAD), BF16),
                        pltpu.VMEM((2, n_ctx + L, tq), F32)],
        compiler_params=_cparams("arbitrary", "arbitrary"),
        name="mla_attn",
    )(q, k_ctx, k_lat, v_ctx, v_lat, x, g1, w_o)


def _route_kernel(x_ref, g_ref, sh_ref, sc_ref, wr_ref, u_ref, slot_ref, aff_ref, slott_ref,
                  afft_ref, tri_ref, *, L, tl, cap):
    b = pl.program_id(0)
    t = pl.program_id(1)
    ne = N_EXPERTS

    @pl.when((b == 0) & (t == 0))
    def _():
        r = lax.broadcasted_iota(jnp.int32, (PREFIX_BLOCK, PREFIX_BLOCK), 0)
        c = lax.broadcasted_iota(jnp.int32, (PREFIX_BLOCK, PREFIX_BLOCK), 1)
        tri_ref[...] = jnp.where(r < c, 1.0, 0.0).astype(BF16)

    u = _norm_modulate(x_ref[0], g_ref[...], sh_ref[0], sc_ref[0]).astype(BF16)
    u_ref[0] = u
    logits = _dot_nt(wr_ref[...], u)
    e = jnp.exp(logits - jnp.max(logits, axis=0, keepdims=True))
    t0 = pl.multiple_of(t * tl, tl)
    afft_ref[:, pl.ds(t0, tl)] = e / jnp.sum(e, axis=0, keepdims=True)

    @pl.when(t == pl.num_programs(1) - 1)
    def _():
        aff = afft_ref[...]

        def as_f32(bits):
            return lax.bitcast_convert_type(bits, F32)

        def count_ge(th):
            return jnp.sum(jnp.where(aff >= th, 1.0, 0.0), axis=1, keepdims=True)

        def body(_, carry):
            lo, hi = carry
            mid = lo + ((hi - lo) >> 1)
            ok = count_ge(as_f32(mid)) >= cap
            return jnp.where(ok, mid, lo), jnp.where(ok, hi, mid)

        lo0 = jnp.zeros((ne, 1), jnp.int32)
        hi0 = jnp.full((ne, 1), 0x7F800000, jnp.int32)
        lo_bits, hi_bits = lax.fori_loop(0, 31, body, (lo0, hi0))
        gt = aff >= as_f32(hi_bits)
        eq = (aff >= as_f32(lo_bits)) & jnp.logical_not(gt)
        need = cap - jnp.sum(jnp.where(gt, 1.0, 0.0), axis=1, keepdims=True)

        def prefix_count(mask):
            ones = jnp.where(mask, 1.0, 0.0)
            base = jnp.zeros((ne, 1), F32)
            parts = []
            for j in range(L // PREFIX_BLOCK):
                blk = ones[:, j * PREFIX_BLOCK:(j + 1) * PREFIX_BLOCK]
                parts.append(_dot(blk.astype(BF16), tri_ref[...]) + base)
                base = base + jnp.sum(blk, axis=1, keepdims=True)
            return parts[0] if len(parts) == 1 else jnp.concatenate(parts, axis=1)

        sel = gt | (eq & (prefix_count(eq) < need))
        pos = prefix_count(sel)
        slot = jnp.where(sel, pos, -1.0)
        slot_i = slot.astype(jnp.int32)
        for ex in range(ne):
            slot_ref[0, ex] = slot_i[ex:ex + 1, :]
            aff_ref[0, ex] = aff[ex:ex + 1, :]
        padded = jnp.concatenate([slot, jnp.full((LANES - ne, L), -1.0, F32)], axis=0)
        slott_ref[0] = padded.T


def _route(x, g, sh, sc, per_sample, w_router_t, cap, tl):
    n, L, d = x.shape
    ne = N_EXPERTS
    vec = lambda b, t: (0, 0)
    samp4 = lambda b, t: (b, 0, 0, 0)
    return pl.pallas_call(
        functools.partial(_route_kernel, L=L, tl=tl, cap=cap),
        grid=(n, L // tl),
        in_specs=[
            pl.BlockSpec((1, tl, d), lambda b, t: (b, t, 0)),
            pl.BlockSpec((1, d), vec),
            _mod_spec(per_sample, d),
            _mod_spec(per_sample, d),
            pl.BlockSpec((ne, d), vec),
        ],
        out_specs=[
            pl.BlockSpec((1, tl, d), lambda b, t: (b, t, 0)),
            pl.BlockSpec((1, ne, 1, L), samp4),
            pl.BlockSpec((1, ne, 1, L), samp4),
            pl.BlockSpec((1, L, LANES), lambda b, t: (b, 0, 0)),
        ],
        out_shape=[
            jax.ShapeDtypeStruct((n, L, d), BF16),
            jax.ShapeDtypeStruct((n, ne, 1, L), jnp.int32),
            jax.ShapeDtypeStruct((n, ne, 1, L), F32),
            jax.ShapeDtypeStruct((n, L, LANES), F32),
        ],
        scratch_shapes=[pltpu.VMEM((ne, L), F32), pltpu.VMEM((PREFIX_BLOCK, PREFIX_BLOCK), BF16)],
        compiler_params=_cparams("arbitrary", "arbitrary"),
        name="route",
    )(x, g, sh, sc, w_router_t)


def _expert_kernel(u_ref, slot_ref, aff_ref, w1_ref, w3_ref, w2_ref, y_ref, w1b, w3b, w2b, *, nb, cap):
    @pl.when(pl.program_id(1) == 0)
    def _():
        w1b[...] = w1_ref[0, 0].astype(BF16)
        w3b[...] = w3_ref[0, 0].astype(BF16)
        w2b[...] = w2_ref[0, 0].astype(BF16)

    L = u_ref.shape[1]
    slot_iota = lax.broadcasted_iota(jnp.int32, (cap, L), 0)
    xs, gates = [], []
    for s in range(nb):
        hit = slot_ref[s, 0] == slot_iota
        xs.append(_dot(jnp.where(hit, 1.0, 0.0).astype(BF16), u_ref[s]).astype(BF16))
        gates.append(jnp.sum(jnp.where(hit, aff_ref[s, 0], 0.0), axis=1, keepdims=True))
    xg = xs[0] if nb == 1 else jnp.concatenate(xs, axis=0)
    gate = gates[0] if nb == 1 else jnp.concatenate(gates, axis=0)
    h1 = _dot(xg, w1b[...])
    hid = (h1 * _sigmoid(h1) * _dot(xg, w3b[...])).astype(BF16)
    y = (_dot(hid, w2b[...]) * gate).astype(BF16)
    for s in range(nb):
        y_ref[0, s] = y[s * cap:(s + 1) * cap]


def _experts(u, slot, aff, layer, w1, w3, w2, cap, nb):
    n, L, d = u.shape
    _, ne, _, ff = w1.shape
    wspec = lambda shape: pl.BlockSpec((1, 1) + shape, lambda e, b: (layer, e, 0, 0))
    return pl.pallas_call(
        functools.partial(_expert_kernel, nb=nb, cap=cap),
        grid=(ne, n // nb),
        in_specs=[
            pl.BlockSpec((nb, L, d), lambda e, b: (b, 0, 0)),
            pl.BlockSpec((nb, 1, 1, L), lambda e, b: (b, e, 0, 0)),
            pl.BlockSpec((nb, 1, 1, L), lambda e, b: (b, e, 0, 0)),
            wspec((d, ff)), wspec((d, ff)), wspec((ff, d)),
        ],
        out_specs=pl.BlockSpec((1, nb, cap, d), lambda e, b: (e, b, 0, 0)),
        out_shape=jax.ShapeDtypeStruct((ne, n, cap, d), BF16),
        scratch_shapes=[pltpu.VMEM((d, ff), BF16), pltpu.VMEM((d, ff), BF16), pltpu.VMEM((ff, d), BF16)],
        compiler_params=_cparams("arbitrary", "arbitrary"),
        name="experts",
    )(u, slot, aff, w1, w3, w2)


def _combine_kernel(*refs, cap, final):
    if final:
        y_ref, st_ref, x_ref, g2_ref, fg_ref, o_ref = refs
    else:
        y_ref, st_ref, x_ref, g2_ref, o_ref = refs
    tl = x_ref.shape[1]
    d = x_ref.shape[2]
    st = st_ref[0]
    lane = lax.broadcasted_iota(jnp.int32, (tl, cap), 1).astype(F32)
    onehot = jnp.concatenate(
        [jnp.where(st[:, e:e + 1] == lane, 1.0, 0.0).astype(BF16) for e in range(N_EXPERTS)], axis=1)
    moe = _dot(onehot, y_ref[...].reshape(N_EXPERTS * cap, d))
    out = x_ref[0] + g2_ref[0] * moe
    if final:
        out = _rms(out, fg_ref[...])
    o_ref[0] = out


def _combine(y, slot_t, x, g2, per_sample, cap, tl, final_g=None):
    n, L, d = x.shape
    ne = N_EXPERTS
    final = final_g is not None
    in_specs = [
        pl.BlockSpec((ne, 1, cap, d), lambda b, t: (0, b, 0, 0)),
        pl.BlockSpec((1, tl, LANES), lambda b, t: (b, t, 0)),
        pl.BlockSpec((1, tl, d), lambda b, t: (b, t, 0)),
        _mod_spec(per_sample, d),
    ]
    args = [y, slot_t, x, g2]
    if final:
        in_specs += [pl.BlockSpec((1, d), lambda b, t: (0, 0))]
        args += [final_g.reshape(1, d)]
    return pl.pallas_call(
        functools.partial(_combine_kernel, cap=cap, final=final),
        grid=(n, L // tl),
        in_specs=in_specs,
        out_specs=pl.BlockSpec((1, tl, d), lambda b, t: (b, t, 0)),
        out_shape=jax.ShapeDtypeStruct((n, L, d), F32),
        compiler_params=_cparams("arbitrary", "arbitrary"),
        name="combine",
    )(*args)


def _ec_moe_layer(x, norm_g, sh, sc, g2, per_sample, layer, w_router, w1, w3, w2, tl, nb, final_g=None):
    n, L, d = x.shape
    cap = EC_CAPACITY_FACTOR * L // N_EXPERTS
    u, slot, aff, slot_t = _route(x, norm_g.reshape(1, d), sh, sc, per_sample,
                                  w_router.T.astype(BF16), cap, tl)
    y = _experts(u, slot, aff, layer, w1, w3, w2, cap, nb)
    return _combine(y, slot_t, x, g2, per_sample, cap, tl, final_g)


def kernel(x, c, ctx, c_ctx, mod_w, mod_b, norm1_g, norm2_g, ab_w_in, ab_conv_w, ab_conv_b, ab_ln_g, ab_ln_b, ab_w_out, mla_w_in, mla_q_norm_g, mla_kv_norm_g, mla_w_uq, mla_w_ukv, mla_w_o, moe_w_router, moe_w1, moe_w3, moe_w2, final_g):
    n, L, d = x.shape
    n_ctx = ctx.shape[1]
    depth = mod_w.shape[0]
    assert n < MOD_ROWS
    cvecs = jnp.concatenate([c, c_ctx[None, :], jnp.zeros((MOD_ROWS - n - 1, d), F32)], axis=0)
    mod = _modulation(cvecs, mod_w, mod_b)

    def mods(i, ctx_stream):
        rows = mod[i, n:n + 1] if ctx_stream else mod[i, :n]
        return [rows[:, k * d:(k + 1) * d].reshape(-1, 1, d) for k in range(N_MOD)]

    x_lat, x_ctx = x, ctx
    for i in range(depth):
        ctx_out = i < depth - 1
        last = i == depth - 1
        j = i // 2
        sh1, sc1, g1, sh2, sc2, g2 = mods(i, False)
        csh1, csc1, cg1, csh2, csc2, cg2 = mods(i, True)
        if i % 2 == 0:
            ab = (ab_w_in[j], ab_conv_w[j], ab_conv_b[j], ab_ln_g[j], ab_ln_b[j], ab_w_out[j])
            x_lat = _conv_fourier_layer(x_lat, norm1_g[i], sh1, sc1, g1, True, *ab, tl=512, tq=256)
            if ctx_out:
                x_ctx = _conv_fourier_layer(x_ctx, norm1_g[i], csh1, csc1, cg1, False, *ab,
                                            tl=n_ctx, tq=n_ctx)
        else:
            assert not ctx_out, "context-stream attention output is only needed for deeper stacks"
            w_in_ext, wq, wkv = _mla_weights(mla_w_in[j], mla_w_uq[j], mla_w_ukv[j])
            cos_t, sin_t = _rope_tables(L)
            g = norm1_g[i].reshape(1, d)
            q_g = mla_q_norm_g[j].reshape(1, Q_LORA)
            kv_g = mla_kv_norm_g[j].reshape(1, KV_LORA)
            q, k_lat, v_lat = _mla_proj(x_lat, g, sh1, sc1, True, w_in_ext, q_g, kv_g, wq, wkv,
                                        cos_t, sin_t, True, 512)
            k_ctx, v_ctx = _mla_proj(x_ctx, g, csh1, csc1, False,
                                     w_in_ext[:, :Q_LORA + KV_LORA + HEAD_PAD], None, kv_g, None, wkv,
                                     None, None, False, n_ctx)
            x_lat = _mla_attn(q, k_ctx, k_lat, v_ctx, v_lat, x_lat, g1, mla_w_o[j].astype(BF16), 512)
        moe = (i, moe_w_router[i], moe_w1, moe_w3, moe_w2)
        x_lat = _ec_moe_layer(x_lat, norm2_g[i], sh2, sc2, g2, True, *moe, tl=512, nb=2,
                              final_g=final_g if last else None)
        if ctx_out:
            x_ctx = _ec_moe_layer(x_ctx, norm2_g[i], csh2, csc2, cg2, False, *moe, tl=n_ctx, nb=n)
    return x_lat
```

```python
import functools

import numpy as np
import jax
import jax.numpy as jnp
from jax import lax
from jax.experimental import pallas as pl
from jax.experimental.pallas import tpu as pltpu

F32 = jnp.float32
BF16 = jnp.bfloat16

NORM_EPS = 1e-6
N_MOD = 6
GRID_W = 64
CONV_WIDTH = 31
CONV_HALO = 16
CONV_TAP_GROUPS = 4
FOURIER_GROUPS = 4
MLA_HEADS = 16
QK_NOPE = 64
QK_ROPE = 32
V_HEAD = 64
V_ROWS = 2 * V_HEAD
QK_HEAD = QK_NOPE + QK_ROPE
HEAD_PAD = 128
Q_LORA = 256
KV_LORA = 128
ROPE_BASE = 10000.0
SCORE_SCALE = float(QK_HEAD ** -0.5 * np.log2(np.e))
KEY_CHUNK = 512
N_EXPERTS = 16
PREFIX_BLOCK = 256
EC_CAPACITY_FACTOR = 2
MOD_ROWS = 24
LANES = 128
VMEM_LIMIT = 56 * 1024 * 1024


def _cparams(*sem):
    return pltpu.CompilerParams(dimension_semantics=sem, vmem_limit_bytes=VMEM_LIMIT)


def _dot(a, b):
    return jnp.dot(a, b, preferred_element_type=F32)


def _dot_nt(a, b):
    return lax.dot_general(a, b, (((1,), (1,)), ((), ())), preferred_element_type=F32)


def _sigmoid(x):
    return 1.0 / (1.0 + jnp.exp(-x))


def _norm_modulate(x, g, shift, scale):
    ms = jnp.mean(x * x, axis=-1, keepdims=True)
    y = x * lax.rsqrt(ms + NORM_EPS) * g
    return y * (1.0 + scale) + shift


def _mod_kernel(c_ref, w_ref, b_ref, o_ref):
    c = c_ref[...]
    s = (c * _sigmoid(c)).astype(BF16)
    o_ref[0] = _dot(s, w_ref[0].astype(BF16)) + b_ref[0]


def _modulation(cvecs, mod_w, mod_b):
    depth, d, nd = mod_w.shape
    tn = 1024
    return pl.pallas_call(
        _mod_kernel,
        grid=(depth, nd // tn),
        in_specs=[
            pl.BlockSpec((MOD_ROWS, d), lambda i, j: (0, 0)),
            pl.BlockSpec((1, d, tn), lambda i, j: (i, 0, j)),
            pl.BlockSpec((1, 1, tn), lambda i, j: (i, 0, j)),
        ],
        out_specs=pl.BlockSpec((1, MOD_ROWS, tn), lambda i, j: (i, 0, j)),
        out_shape=jax.ShapeDtypeStruct((depth, MOD_ROWS, nd), F32),
        compiler_params=_cparams("arbitrary", "arbitrary"),
        name="modulation",
    )(cvecs, mod_w, mod_b.reshape(depth, 1, nd))


def _mod_spec(per_sample, d):
    if per_sample:
        return pl.BlockSpec((1, 1, d), lambda b, t: (b, 0, 0))
    return pl.BlockSpec((1, 1, d), lambda b, t: (0, 0, 0))


def _ab_in_kernel(x_ref, g_ref, sh_ref, sc_ref, w_ref, cs_ref, a_ref, y_ref, *, conv_ch, group_ch):
    u = _norm_modulate(x_ref[0], g_ref[...], sh_ref[0], sc_ref[0]).astype(BF16)
    z = _dot(u, w_ref[...])
    a_ref[0] = z[:, :conv_ch] * _sigmoid(z[:, conv_ch:2 * conv_ch])
    uf = z[:, 2 * conv_ch:].astype(BF16)
    for g in range(FOURIER_GROUPS):
        yg = _dot(uf[:, g * group_ch:(g + 1) * group_ch], cs_ref[...])
        y_ref[0, 0, :, g * group_ch:(g + 1) * group_ch] = yg[:, :group_ch].astype(BF16)
        y_ref[0, 1, :, g * group_ch:(g + 1) * group_ch] = yg[:, group_ch:].astype(BF16)


def _ab_in(x, g, sh, sc, per_sample, w_in, cs, conv_ch, tl):
    n, L, d = x.shape
    ab_in = w_in.shape[1]
    four_ch = ab_in - 2 * conv_ch
    group_ch = four_ch // FOURIER_GROUPS
    kern = functools.partial(_ab_in_kernel, conv_ch=conv_ch, group_ch=group_ch)
    return pl.pallas_call(
        kern,
        grid=(n, L // tl),
        in_specs=[
            pl.BlockSpec((1, tl, d), lambda b, t: (b, t, 0)),
            pl.BlockSpec((1, d), lambda b, t: (0, 0)),
            _mod_spec(per_sample, d),
            _mod_spec(per_sample, d),
            pl.BlockSpec((d, ab_in), lambda b, t: (0, 0)),
            pl.BlockSpec(cs.shape, lambda b, t: (0, 0)),
        ],
        out_specs=[
            pl.BlockSpec((1, tl, conv_ch), lambda b, t: (b, t, 0)),
            pl.BlockSpec((1, 2, tl, four_ch), lambda b, t: (b, 0, t, 0)),
        ],
        out_shape=[
            jax.ShapeDtypeStruct((n, L, conv_ch), F32),
            jax.ShapeDtypeStruct((n, 2, L, four_ch), BF16),
        ],
        compiler_params=_cparams("arbitrary", "arbitrary"),
        name="ab_in",
    )(x, g, sh, sc, w_in, cs)


def _ab_out_kernel(a_ref, y_ref, d_ref, x_ref, g1_ref, cw_ref, cb_ref, lg_ref, lb_ref, wo_ref,
                   o_ref, apad_ref, act_ref, *, L, tq, conv_ch):
    t = pl.program_id(1)

    @pl.when(t == 0)
    def _():
        apad_ref[0:CONV_HALO, :] = jnp.zeros((CONV_HALO, conv_ch), F32)
        apad_ref[CONV_HALO:CONV_HALO + L, :] = a_ref[0]
        apad_ref[CONV_HALO + L:CONV_HALO + L + CONV_HALO, :] = jnp.zeros((CONV_HALO, conv_ch), F32)

    t0 = pl.multiple_of(t * tq, tq)
    win = tq + 8 * CONV_TAP_GROUPS
    for c in range(conv_ch // LANES):
        cols = slice(c * LANES, (c + 1) * LANES)
        full = apad_ref[pl.ds(t0, win), cols]
        acc = jnp.zeros((tq, LANES), F32) + cb_ref[:, cols]
        for r in range(8):
            shifted = full if r == 0 else pltpu.roll(full, win - r, axis=0)
            for j in range(CONV_TAP_GROUPS):
                m = 8 * j + r
                if m == 0:
                    continue
                acc = acc + shifted[8 * j:8 * j + tq, :] * cw_ref[m:m + 1, cols]
        act_ref[:, cols] = acc
    four = _dot(d_ref[...], y_ref[0])
    conv = act_ref[...]
    mu = jnp.mean(conv, axis=-1, keepdims=True)
    cen = conv - mu
    var = jnp.mean(cen * cen, axis=-1, keepdims=True)
    ln = cen * lax.rsqrt(var + NORM_EPS) * lg_ref[...] + lb_ref[...]
    act = (ln * _sigmoid(ln)).astype(BF16)
    mix = _dot(act, wo_ref[0:conv_ch, :]) + _dot(four.astype(BF16), wo_ref[conv_ch:, :])
    o_ref[0] = x_ref[0] + g1_ref[0] * mix


def _ab_out(a, ystack, dmat, x, g1, per_sample, conv_w, conv_b, ln_g, ln_b, w_out, tq):
    n, L, d = x.shape
    conv_ch = a.shape[-1]
    four_ch = ystack.shape[-1]
    kern = functools.partial(_ab_out_kernel, L=L, tq=tq, conv_ch=conv_ch)
    vec = lambda b, t: (0, 0)
    return pl.pallas_call(
        kern,
        grid=(n, L // tq),
        in_specs=[
            pl.BlockSpec((1, L, conv_ch), lambda b, t: (b, 0, 0)),
            pl.BlockSpec((1, 2 * L, four_ch), lambda b, t: (b, 0, 0)),
            pl.BlockSpec((tq, 2 * L), lambda b, t: (t, 0)),
            pl.BlockSpec((1, tq, d), lambda b, t: (b, t, 0)),
            _mod_spec(per_sample, d),
            pl.BlockSpec(conv_w.shape, vec),
            pl.BlockSpec((1, conv_ch), vec),
            pl.BlockSpec((1, conv_ch), vec),
            pl.BlockSpec((1, conv_ch), vec),
            pl.BlockSpec(w_out.shape, vec),
        ],
        out_specs=pl.BlockSpec((1, tq, d), lambda b, t: (b, t, 0)),
        out_shape=jax.ShapeDtypeStruct((n, L, d), F32),
        scratch_shapes=[
            pltpu.VMEM((L + 2 * CONV_HALO, conv_ch), F32),
            pltpu.VMEM((tq, conv_ch), F32),
        ],
        compiler_params=_cparams("arbitrary", "arbitrary"),
        name="ab_out",
    )(a, ystack, dmat, x, g1, conv_w, conv_b, ln_g, ln_b, w_out)


def _dft_tables(L, group_ch):
    k = np.arange(L, dtype=np.int64)
    ang = 2.0 * np.pi * ((k[:, None] * k[None, :]) % L).astype(np.float64) / L
    dmat = np.concatenate([np.cos(ang), -np.sin(ang)], axis=1) / np.sqrt(L)
    m = np.arange(group_ch, dtype=np.int64)
    angc = 2.0 * np.pi * ((m[:, None] * m[None, :]) % group_ch).astype(np.float64) / group_ch
    cs = np.concatenate([np.cos(angc), np.sin(angc)], axis=1) / np.sqrt(group_ch)
    return jnp.asarray(dmat.astype(np.float32)).astype(BF16), jnp.asarray(cs.astype(np.float32)).astype(BF16)


def _conv_fourier_layer(x, norm_g, sh, sc, g1, per_sample, w_in, conv_w, conv_b, ln_g, ln_b, w_out, tl, tq):
    n, L, d = x.shape
    conv_ch = conv_w.shape[-1]
    four_ch = w_in.shape[1] - 2 * conv_ch
    dmat, cs = _dft_tables(L, four_ch // FOURIER_GROUPS)
    a, y = _ab_in(x, norm_g.reshape(1, d), sh, sc, per_sample, w_in.astype(BF16), cs, conv_ch, tl)
    cw = jnp.concatenate([jnp.zeros((1, conv_ch), F32), conv_w], axis=0)
    return _ab_out(a, y.reshape(n, 2 * L, four_ch), dmat, x, g1, per_sample, cw,
                   conv_b.reshape(1, conv_ch), ln_g.reshape(1, conv_ch), ln_b.reshape(1, conv_ch),
                   w_out.astype(BF16), tq)


def _rope_tables(L):
    rows = L // GRID_W
    row = np.repeat(np.arange(rows, dtype=np.float32), GRID_W)
    col = np.tile(np.arange(GRID_W, dtype=np.float32), rows)
    nf = QK_ROPE // 4
    inv_freq = (np.float32(ROPE_BASE) ** (-np.arange(nf, dtype=np.float32) / np.float32(nf))).astype(np.float32)
    ang = np.concatenate([row[:, None] * inv_freq, col[:, None] * inv_freq], axis=-1).astype(np.float32)
    cos = np.repeat(np.cos(ang.astype(np.float64)), 2, axis=-1)
    sin = np.repeat(np.sin(ang.astype(np.float64)), 2, axis=-1)
    cos_t = np.zeros((L, HEAD_PAD), np.float32)
    sin_t = np.zeros((L, HEAD_PAD), np.float32)
    cos_t[:, :QK_NOPE] = 1.0
    cos_t[:, QK_NOPE:QK_HEAD] = cos
    sin_t[:, QK_NOPE:QK_HEAD] = sin
    return jnp.asarray(cos_t), jnp.asarray(sin_t)


def _pair_swap(w):
    w2 = w.reshape(w.shape[:-1] + (QK_ROPE // 2, 2))
    return jnp.stack([-w2[..., 1], w2[..., 0]], axis=-1).reshape(w.shape)


def _mla_weights(w_in, w_uq, w_ukv):
    d = w_in.shape[0]
    h = MLA_HEADS
    zeros = lambda *s: jnp.zeros(s, F32)
    w_kr = w_in[:, Q_LORA + KV_LORA:]
    kr_a = jnp.concatenate([zeros(d, QK_NOPE), w_kr, zeros(d, HEAD_PAD - QK_HEAD)], axis=1)
    kr_b = jnp.concatenate([zeros(d, QK_NOPE), _pair_swap(w_kr), zeros(d, HEAD_PAD - QK_HEAD)], axis=1)
    w_in_ext = jnp.concatenate([w_in[:, :Q_LORA + KV_LORA], kr_a, kr_b], axis=1)
    wq = w_uq.reshape(Q_LORA, h, QK_HEAD)
    pad = zeros(Q_LORA, h, HEAD_PAD - QK_HEAD)
    wq_a = jnp.concatenate([wq, pad], axis=-1).reshape(Q_LORA, h * HEAD_PAD)
    wq_b = jnp.concatenate([zeros(Q_LORA, h, QK_NOPE), _pair_swap(wq[..., QK_NOPE:]), pad],
                           axis=-1).reshape(Q_LORA, h * HEAD_PAD)
    wkv = w_ukv.reshape(KV_LORA, h, QK_NOPE + V_HEAD)
    wk = jnp.concatenate([wkv[..., :QK_NOPE], zeros(KV_LORA, h, HEAD_PAD - QK_NOPE)],
                         axis=-1).reshape(KV_LORA, h * HEAD_PAD)
    wv = wkv[..., QK_NOPE:].reshape(KV_LORA, h * V_HEAD)
    return (w_in_ext.astype(BF16), jnp.concatenate([wq_a, wq_b], axis=1).astype(BF16),
            jnp.concatenate([wk, wv], axis=1).astype(BF16))


def _rms(x, g):
    return x * lax.rsqrt(jnp.mean(x * x, axis=-1, keepdims=True) + NORM_EPS) * g


def _mla_proj_kernel(*refs, latent):
    if latent:
        (x_ref, g_ref, sh_ref, sc_ref, win_ref, qg_ref, kvg_ref, wq_ref, wkv_ref, cos_ref, sin_ref,
         q_ref, k_ref, v_ref) = refs
    else:
        (x_ref, g_ref, sh_ref, sc_ref, win_ref, kvg_ref, wkv_ref, k_ref, v_ref) = refs
    hp = MLA_HEADS * HEAD_PAD
    u = _norm_modulate(x_ref[0], g_ref[...], sh_ref[0], sc_ref[0]).astype(BF16)
    z = _dot(u, win_ref[...])
    ckv = _rms(z[:, Q_LORA:Q_LORA + KV_LORA], kvg_ref[...]).astype(BF16)
    kv = _dot(ckv, wkv_ref[...])
    kr = z[:, Q_LORA + KV_LORA:Q_LORA + KV_LORA + HEAD_PAD]
    if latent:
        cos = cos_ref[...]
        sin = sin_ref[...]
        kr = kr * cos + z[:, Q_LORA + KV_LORA + HEAD_PAD:] * sin
        cq = _rms(z[:, :Q_LORA], qg_ref[...]).astype(BF16)
        qq = _dot(cq, wq_ref[...])
        for h in range(MLA_HEADS):
            cols = slice(h * HEAD_PAD, (h + 1) * HEAD_PAD)
            qb = qq[:, hp + h * HEAD_PAD:hp + (h + 1) * HEAD_PAD]
            q_ref[0, h] = ((qq[:, cols] * cos + qb * sin) * SCORE_SCALE).T.astype(BF16)
    for h in range(MLA_HEADS):
        cols = slice(h * HEAD_PAD, (h + 1) * HEAD_PAD)
        k_ref[0, h] = (kv[:, cols] + kr).astype(BF16)
    for hpair in range(MLA_HEADS // 2):
        v_ref[0, hpair] = kv[:, hp + hpair * 2 * V_HEAD:hp + (hpair + 1) * 2 * V_HEAD].T.astype(BF16)


def _mla_proj(x, g, sh, sc, per_sample, w_in_ext, q_g, kv_g, wq, wkv, cos_t, sin_t, latent, tl):
    n, L, d = x.shape
    vec = lambda b, t: (0, 0)
    row = lambda b, t: (b, t, 0)
    in_specs = [
        pl.BlockSpec((1, tl, d), row),
        pl.BlockSpec((1, d), vec),
        _mod_spec(per_sample, d),
        _mod_spec(per_sample, d),
        pl.BlockSpec(w_in_ext.shape, vec),
    ]
    args = [x, g, sh, sc, w_in_ext]
    if latent:
        in_specs += [pl.BlockSpec((1, Q_LORA), vec)]
        args += [q_g]
    in_specs += [pl.BlockSpec((1, KV_LORA), vec)]
    args += [kv_g]
    if latent:
        in_specs += [pl.BlockSpec(wq.shape, vec)]
        args += [wq]
    in_specs += [pl.BlockSpec(wkv.shape, vec)]
    args += [wkv]
    nh, nhp = MLA_HEADS, MLA_HEADS // 2
    head_row = lambda b, t: (b, 0, t, 0)
    head_col = lambda b, t: (b, 0, 0, t)
    out_specs = [pl.BlockSpec((1, nh, tl, HEAD_PAD), head_row), pl.BlockSpec((1, nhp, V_ROWS, tl), head_col)]
    out_shape = [jax.ShapeDtypeStruct((n, nh, L, HEAD_PAD), BF16),
                 jax.ShapeDtypeStruct((n, nhp, V_ROWS, L), BF16)]
    if latent:
        in_specs += [pl.BlockSpec((tl, HEAD_PAD), lambda b, t: (t, 0))] * 2
        args += [cos_t, sin_t]
        out_specs = [pl.BlockSpec((1, nh, HEAD_PAD, tl), head_col)] + out_specs
        out_shape = [jax.ShapeDtypeStruct((n, nh, HEAD_PAD, L), BF16)] + out_shape
    return pl.pallas_call(
        functools.partial(_mla_proj_kernel, latent=latent),
        grid=(n, L // tl),
        in_specs=in_specs,
        out_specs=out_specs,
        out_shape=out_shape,
        compiler_params=_cparams("arbitrary", "arbitrary"),
        name="mla_proj_lat" if latent else "mla_proj_ctx",
    )(*args)


def _mla_attn_kernel(q_ref, kc_ref, kl_ref, vc_ref, vl_ref, x_ref, g1_ref, wo_ref, o_ref, oh_ref, s_ref,
                     *, n_ctx, n_lat):
    tq = q_ref.shape[3]
    chunks = [(True, 0, n_ctx, 0)]
    chunks += [(False, k0, KEY_CHUNK, n_ctx + k0) for k0 in range(0, n_lat, KEY_CHUNK)]
    even_rows = lax.broadcasted_iota(jnp.int32, (2 * V_HEAD, tq), 0) < V_HEAD

    def scores(h, par):
        qt = q_ref[0, h]
        mx = jnp.full((1, tq), -jnp.inf, F32)
        for is_ctx, k0, kn, r0 in chunks:
            k_ref = kc_ref if is_ctx else kl_ref
            s = _dot(k_ref[0, h, k0:k0 + kn, :], qt)
            s_ref[par, r0:r0 + kn, :] = s
            mx = jnp.maximum(mx, jnp.max(s, axis=0, keepdims=True))
        return mx

    def values(hpair, par, mx):
        den = jnp.zeros((1, tq), F32)
        acc = jnp.zeros((V_ROWS, tq), F32)
        for is_ctx, k0, kn, r0 in chunks:
            v_ref = vc_ref if is_ctx else vl_ref
            p = jnp.exp2(s_ref[par, r0:r0 + kn, :] - mx)
            den = den + jnp.sum(p, axis=0, keepdims=True)
            acc = acc + _dot(v_ref[0, hpair, :, k0:k0 + kn], p.astype(BF16))
        return acc * (1.0 / den)

    def head_pair(hpair, mx_even, last):
        mx_odd = scores(2 * hpair + 1, 1)
        out_even = values(hpair, 0, mx_even)
        mx_next = mx_even if last else scores(2 * hpair + 2, 0)
        out_odd = values(hpair, 1, mx_odd)
        oh_ref[hpair] = jnp.where(even_rows, out_even, out_odd).T.astype(BF16)
        return mx_next

    n_pairs = MLA_HEADS // 2
    mx0 = lax.fori_loop(0, n_pairs - 1, lambda i, mx: head_pair(i, mx, False), scores(0, 0))
    head_pair(n_pairs - 1, mx0, True)
    heads = jnp.concatenate([oh_ref[i] for i in range(n_pairs)], axis=1)
    o_ref[0] = x_ref[0] + g1_ref[0] * _dot(heads, wo_ref[...])


def _mla_attn(q, k_ctx, k_lat, v_ctx, v_lat, x, g1, w_o, tq):
    n, L, d = x.shape
    n_ctx = k_ctx.shape[2]
    nh, nhp = MLA_HEADS, MLA_HEADS // 2
    samp = lambda b, t: (b, 0, 0, 0)
    row = lambda b, t: (b, t, 0)
    assert L % KEY_CHUNK == 0
    return pl.pallas_call(
        functools.partial(_mla_attn_kernel, n_ctx=n_ctx, n_lat=L),
        grid=(n, L // tq),
        in_specs=[
            pl.BlockSpec((1, nh, HEAD_PAD, tq), lambda b, t: (b, 0, 0, t)),
            pl.BlockSpec((1, nh, n_ctx, HEAD_PAD), samp),
            pl.BlockSpec((1, nh, L, HEAD_PAD), samp),
            pl.BlockSpec((1, nhp, V_ROWS, n_ctx), samp),
            pl.BlockSpec((1, nhp, V_ROWS, L), samp, pipeline_mode=pl.Buffered(1)),
            pl.BlockSpec((1, tq, d), row),
            _mod_spec(True, d),
            pl.BlockSpec(w_o.shape, lambda b, t: (0, 0), pipeline_mode=pl.Buffered(1)),
        ],
        out_specs=pl.BlockSpec((1, tq, d), row),
        out_shape=jax.ShapeDtypeStruct((n, L, d), F32),
        scratch_shapes=[pltpu.VMEM((nhp, tq, 2 * V_HEAD), BF16),
                        pltpu.VMEM((2, n_ctx + L, tq), F32)],
        compiler_params=_cparams("arbitrary", "arbitrary"),
        name="mla_attn",
    )(q, k_ctx, k_lat, v_ctx, v_lat, x, g1, w_o)


def _route_kernel(x_ref, g_ref, sh_ref, sc_ref, wr_ref, u_ref, slot_ref, aff_ref, slott_ref,
                  afft_ref, tri_ref, *, L, tl, cap, gs):
    s = pl.program_id(1)
    t = pl.program_id(2)
    ne = N_EXPERTS
    rows = gs * ne

    @pl.when((pl.program_id(0) == 0) & (s == 0) & (t == 0))
    def _():
        r = lax.broadcasted_iota(jnp.int32, (PREFIX_BLOCK, PREFIX_BLOCK), 0)
        c = lax.broadcasted_iota(jnp.int32, (PREFIX_BLOCK, PREFIX_BLOCK), 1)
        tri_ref[...] = jnp.where(r < c, 1.0, 0.0).astype(BF16)

    u = _norm_modulate(x_ref[0], g_ref[...], sh_ref[0], sc_ref[0]).astype(BF16)
    u_ref[0] = u
    logits = _dot_nt(wr_ref[...], u)
    e = jnp.exp(logits - jnp.max(logits, axis=0, keepdims=True))
    t0 = pl.multiple_of(t * tl, tl)
    afft_ref[pl.ds(pl.multiple_of(s * ne, ne), ne), pl.ds(t0, tl)] = e / jnp.sum(e, axis=0, keepdims=True)

    @pl.when((s == gs - 1) & (t == pl.num_programs(2) - 1))
    def _():
        aff = afft_ref[...]

        def as_f32(bits):
            return lax.bitcast_convert_type(bits, F32)

        def count_ge(th):
            return jnp.sum(jnp.where(aff >= th, 1.0, 0.0), axis=1, keepdims=True)

        def body(_, carry):
            lo, hi = carry
            mid = lo + ((hi - lo) >> 1)
            ok = count_ge(as_f32(mid)) >= cap
            return jnp.where(ok, mid, lo), jnp.where(ok, hi, mid)

        lo0 = jnp.zeros((rows, 1), jnp.int32)
        hi0 = jnp.full((rows, 1), 0x7F800000, jnp.int32)
        lo_bits, hi_bits = lax.fori_loop(0, 31, body, (lo0, hi0))
        gt = aff >= as_f32(hi_bits)
        eq = (aff >= as_f32(lo_bits)) & jnp.logical_not(gt)
        need = cap - jnp.sum(jnp.where(gt, 1.0, 0.0), axis=1, keepdims=True)

        def prefix_count(mask):
            ones = jnp.where(mask, 1.0, 0.0)
            base = jnp.zeros((rows, 1), F32)
            parts = []
            for j in range(L // PREFIX_BLOCK):
                blk = ones[:, j * PREFIX_BLOCK:(j + 1) * PREFIX_BLOCK]
                parts.append(_dot(blk.astype(BF16), tri_ref[...]) + base)
                base = base + jnp.sum(blk, axis=1, keepdims=True)
            return parts[0] if len(parts) == 1 else jnp.concatenate(parts, axis=1)

        sel = gt | (eq & (prefix_count(eq) < need))
        pos = prefix_count(sel)
        slot = jnp.where(sel, pos, -1.0)
        slot_i = slot.astype(jnp.int32)
        fill = jnp.full((LANES - ne, L), -1.0, F32)
        for si in range(gs):
            for ex in range(ne):
                r = si * ne + ex
                slot_ref[si, ex] = slot_i[r:r + 1, :]
                aff_ref[si, ex] = aff[r:r + 1, :]
            slott_ref[si] = jnp.concatenate([slot[si * ne:(si + 1) * ne], fill], axis=0).T


def _route(x, g, sh, sc, per_sample, w_router_t, cap, tl, gs):
    n, L, d = x.shape
    ne = N_EXPERTS
    vec = lambda gi, s, t: (0, 0)
    row = lambda gi, s, t: (gi * gs + s, t, 0)
    group4 = lambda gi, s, t: (gi, 0, 0, 0)
    if per_sample:
        mod_spec = pl.BlockSpec((1, 1, d), lambda gi, s, t: (gi * gs + s, 0, 0))
    else:
        mod_spec = pl.BlockSpec((1, 1, d), lambda gi, s, t: (0, 0, 0))
    return pl.pallas_call(
        functools.partial(_route_kernel, L=L, tl=tl, cap=cap, gs=gs),
        grid=(n // gs, gs, L // tl),
        in_specs=[
            pl.BlockSpec((1, tl, d), row),
            pl.BlockSpec((1, d), vec),
            mod_spec,
            mod_spec,
            pl.BlockSpec((ne, d), vec),
        ],
        out_specs=[
            pl.BlockSpec((1, tl, d), row),
            pl.BlockSpec((gs, ne, 1, L), group4),
            pl.BlockSpec((gs, ne, 1, L), group4),
            pl.BlockSpec((gs, L, LANES), lambda gi, s, t: (gi, 0, 0)),
        ],
        out_shape=[
            jax.ShapeDtypeStruct((n, L, d), BF16),
            jax.ShapeDtypeStruct((n, ne, 1, L), jnp.int32),
            jax.ShapeDtypeStruct((n, ne, 1, L), F32),
            jax.ShapeDtypeStruct((n, L, LANES), F32),
        ],
        scratch_shapes=[pltpu.VMEM((gs * ne, L), F32), pltpu.VMEM((PREFIX_BLOCK, PREFIX_BLOCK), BF16)],
        compiler_params=_cparams("arbitrary", "arbitrary", "arbitrary"),
        name="route",
    )(x, g, sh, sc, w_router_t)


def _expert_kernel(u_ref, slot_ref, aff_ref, w1_ref, w3_ref, w2_ref, y_ref, w1b, w3b, w2b, *, nb, cap):
    @pl.when(pl.program_id(1) == 0)
    def _():
        w1b[...] = w1_ref[0, 0].astype(BF16)
        w3b[...] = w3_ref[0, 0].astype(BF16)
        w2b[...] = w2_ref[0, 0].astype(BF16)

    L = u_ref.shape[1]
    slot_iota = lax.broadcasted_iota(jnp.int32, (cap, L), 0)
    xs, gates = [], []
    for s in range(nb):
        hit = slot_ref[s, 0] == slot_iota
        xs.append(_dot(jnp.where(hit, 1.0, 0.0).astype(BF16), u_ref[s]).astype(BF16))
        gates.append(jnp.sum(jnp.where(hit, aff_ref[s, 0], 0.0), axis=1, keepdims=True))
    xg = xs[0] if nb == 1 else jnp.concatenate(xs, axis=0)
    gate = gates[0] if nb == 1 else jnp.concatenate(gates, axis=0)
    h1 = _dot(xg, w1b[...])
    hid = (h1 * _sigmoid(h1) * _dot(xg, w3b[...])).astype(BF16)
    y = (_dot(hid, w2b[...]) * gate).astype(BF16)
    for s in range(nb):
        y_ref[0, s] = y[s * cap:(s + 1) * cap]


def _experts(u, slot, aff, layer, w1, w3, w2, cap, nb):
    n, L, d = u.shape
    _, ne, _, ff = w1.shape
    wspec = lambda shape: pl.BlockSpec((1, 1) + shape, lambda e, b: (layer, e, 0, 0))
    return pl.pallas_call(
        functools.partial(_expert_kernel, nb=nb, cap=cap),
        grid=(ne, n // nb),
        in_specs=[
            pl.BlockSpec((nb, L, d), lambda e, b: (b, 0, 0)),
            pl.BlockSpec((nb, 1, 1, L), lambda e, b: (b, e, 0, 0)),
            pl.BlockSpec((nb, 1, 1, L), lambda e, b: (b, e, 0, 0)),
            wspec((d, ff)), wspec((d, ff)), wspec((ff, d)),
        ],
        out_specs=pl.BlockSpec((1, nb, cap, d), lambda e, b: (e, b, 0, 0)),
        out_shape=jax.ShapeDtypeStruct((ne, n, cap, d), BF16),
        scratch_shapes=[pltpu.VMEM((d, ff), BF16), pltpu.VMEM((d, ff), BF16), pltpu.VMEM((ff, d), BF16)],
        compiler_params=_cparams("arbitrary", "arbitrary"),
        name="experts",
    )(u, slot, aff, w1, w3, w2)


def _combine_kernel(*refs, cap, final):
    if final:
        y_ref, st_ref, x_ref, g2_ref, fg_ref, o_ref = refs
    else:
        y_ref, st_ref, x_ref, g2_ref, o_ref = refs
    tl = x_ref.shape[1]
    d = x_ref.shape[2]
    st = st_ref[0]
    lane = lax.broadcasted_iota(jnp.int32, (tl, cap), 1).astype(F32)
    onehot = jnp.concatenate(
        [jnp.where(st[:, e:e + 1] == lane, 1.0, 0.0).astype(BF16) for e in range(N_EXPERTS)], axis=1)
    moe = _dot(onehot, y_ref[...].reshape(N_EXPERTS * cap, d))
    out = x_ref[0] + g2_ref[0] * moe
    if final:
        out = _rms(out, fg_ref[...])
    o_ref[0] = out


def _combine(y, slot_t, x, g2, per_sample, cap, tl, final_g=None):
    n, L, d = x.shape
    ne = N_EXPERTS
    final = final_g is not None
    in_specs = [
        pl.BlockSpec((ne, 1, cap, d), lambda b, t: (0, b, 0, 0)),
        pl.BlockSpec((1, tl, LANES), lambda b, t: (b, t, 0)),
        pl.BlockSpec((1, tl, d), lambda b, t: (b, t, 0)),
        _mod_spec(per_sample, d),
    ]
    args = [y, slot_t, x, g2]
    if final:
        in_specs += [pl.BlockSpec((1, d), lambda b, t: (0, 0))]
        args += [final_g.reshape(1, d)]
    return pl.pallas_call(
        functools.partial(_combine_kernel, cap=cap, final=final),
        grid=(n, L // tl),
        in_specs=in_specs,
        out_specs=pl.BlockSpec((1, tl, d), lambda b, t: (b, t, 0)),
        out_shape=jax.ShapeDtypeStruct((n, L, d), F32),
        compiler_params=_cparams("arbitrary", "arbitrary"),
        name="combine",
    )(*args)


def _ec_moe_layer(x, norm_g, sh, sc, g2, per_sample, layer, w_router, w1, w3, w2, tl, nb, gs, final_g=None):
    n, L, d = x.shape
    cap = EC_CAPACITY_FACTOR * L // N_EXPERTS
    u, slot, aff, slot_t = _route(x, norm_g.reshape(1, d), sh, sc, per_sample,
                                  w_router.T.astype(BF16), cap, tl, gs)
    y = _experts(u, slot, aff, layer, w1, w3, w2, cap, nb)
    return _combine(y, slot_t, x, g2, per_sample, cap, tl, final_g)


def kernel(x, c, ctx, c_ctx, mod_w, mod_b, norm1_g, norm2_g, ab_w_in, ab_conv_w, ab_conv_b, ab_ln_g, ab_ln_b, ab_w_out, mla_w_in, mla_q_norm_g, mla_kv_norm_g, mla_w_uq, mla_w_ukv, mla_w_o, moe_w_router, moe_w1, moe_w3, moe_w2, final_g):
    n, L, d = x.shape
    n_ctx = ctx.shape[1]
    depth = mod_w.shape[0]
    assert n < MOD_ROWS
    cvecs = jnp.concatenate([c, c_ctx[None, :], jnp.zeros((MOD_ROWS - n - 1, d), F32)], axis=0)
    mod = _modulation(cvecs, mod_w, mod_b)

    def mods(i, ctx_stream):
        rows = mod[i, n:n + 1] if ctx_stream else mod[i, :n]
        return [rows[:, k * d:(k + 1) * d].reshape(-1, 1, d) for k in range(N_MOD)]

    x_lat, x_ctx = x, ctx
    for i in range(depth):
        ctx_out = i < depth - 1
        last = i == depth - 1
        j = i // 2
        sh1, sc1, g1, sh2, sc2, g2 = mods(i, False)
        csh1, csc1, cg1, csh2, csc2, cg2 = mods(i, True)
        if i % 2 == 0:
            ab = (ab_w_in[j], ab_conv_w[j], ab_conv_b[j], ab_ln_g[j], ab_ln_b[j], ab_w_out[j])
            x_lat = _conv_fourier_layer(x_lat, norm1_g[i], sh1, sc1, g1, True, *ab, tl=512, tq=256)
            if ctx_out:
                x_ctx = _conv_fourier_layer(x_ctx, norm1_g[i], csh1, csc1, cg1, False, *ab,
                                            tl=n_ctx, tq=n_ctx)
        else:
            assert not ctx_out, "context-stream attention output is only needed for deeper stacks"
            w_in_ext, wq, wkv = _mla_weights(mla_w_in[j], mla_w_uq[j], mla_w_ukv[j])
            cos_t, sin_t = _rope_tables(L)
            g = norm1_g[i].reshape(1, d)
            q_g = mla_q_norm_g[j].reshape(1, Q_LORA)
            kv_g = mla_kv_norm_g[j].reshape(1, KV_LORA)
            q, k_lat, v_lat = _mla_proj(x_lat, g, sh1, sc1, True, w_in_ext, q_g, kv_g, wq, wkv,
                                        cos_t, sin_t, True, 512)
            k_ctx, v_ctx = _mla_proj(x_ctx, g, csh1, csc1, False,
                                     w_in_ext[:, :Q_LORA + KV_LORA + HEAD_PAD], None, kv_g, None, wkv,
                                     None, None, False, n_ctx)
            x_lat = _mla_attn(q, k_ctx, k_lat, v_ctx, v_lat, x_lat, g1, mla_w_o[j].astype(BF16), 512)
        moe = (i, moe_w_router[i], moe_w1, moe_w3, moe_w2)
        x_lat = _ec_moe_layer(x_lat, norm2_g[i], sh2, sc2, g2, True, *moe, tl=512, nb=2, gs=4,
                              final_g=final_g if last else None)
        if ctx_out:
            x_ctx = _ec_moe_layer(x_ctx, norm2_g[i], csh2, csc2, cg2, False, *moe, tl=n_ctx, nb=n, gs=n)
    return x_lat
```

```python
import functools

import numpy as np
import jax
import jax.numpy as jnp
from jax import lax
from jax.experimental import pallas as pl
from jax.experimental.pallas import tpu as pltpu

F32 = jnp.float32
BF16 = jnp.bfloat16

NORM_EPS = 1e-6
N_MOD = 6
GRID_W = 64
CONV_WIDTH = 31
CONV_HALO = 16
CONV_TAP_GROUPS = 4
FOURIER_GROUPS = 4
MLA_HEADS = 16
QK_NOPE = 64
QK_ROPE = 32
V_HEAD = 64
V_ROWS = 2 * V_HEAD
QK_HEAD = QK_NOPE + QK_ROPE
HEAD_PAD = 128
Q_LORA = 256
KV_LORA = 128
ROPE_BASE = 10000.0
SCORE_SCALE = float(QK_HEAD ** -0.5 * np.log2(np.e))
KEY_CHUNK = 512
N_EXPERTS = 16
PREFIX_BLOCK = 256
GATHER_ROWS = 128
SUBLANES = 8
FIRST_STRIDE = 16
EC_CAPACITY_FACTOR = 2
MOD_ROWS = 24
LANES = 128
VMEM_LIMIT = 56 * 1024 * 1024


def _cparams(*sem):
    return pltpu.CompilerParams(dimension_semantics=sem, vmem_limit_bytes=VMEM_LIMIT)


def _dot(a, b):
    return jnp.dot(a, b, preferred_element_type=F32)


def _dot_nt(a, b):
    return lax.dot_general(a, b, (((1,), (1,)), ((), ())), preferred_element_type=F32)


def _sigmoid(x):
    return 1.0 / (1.0 + jnp.exp(-x))


def _norm_modulate(x, g, shift, scale):
    ms = jnp.mean(x * x, axis=-1, keepdims=True)
    y = x * lax.rsqrt(ms + NORM_EPS) * g
    return y * (1.0 + scale) + shift


def _mod_kernel(c_ref, w_ref, b_ref, o_ref):
    c = c_ref[...]
    s = (c * _sigmoid(c)).astype(BF16)
    o_ref[0] = _dot(s, w_ref[0].astype(BF16)) + b_ref[0]


def _modulation(cvecs, mod_w, mod_b):
    depth, d, nd = mod_w.shape
    tn = 1024
    return pl.pallas_call(
        _mod_kernel,
        grid=(depth, nd // tn),
        in_specs=[
            pl.BlockSpec((MOD_ROWS, d), lambda i, j: (0, 0)),
            pl.BlockSpec((1, d, tn), lambda i, j: (i, 0, j)),
            pl.BlockSpec((1, 1, tn), lambda i, j: (i, 0, j)),
        ],
        out_specs=pl.BlockSpec((1, MOD_ROWS, tn), lambda i, j: (i, 0, j)),
        out_shape=jax.ShapeDtypeStruct((depth, MOD_ROWS, nd), F32),
        compiler_params=_cparams("arbitrary", "arbitrary"),
        name="modulation",
    )(cvecs, mod_w, mod_b.reshape(depth, 1, nd))


def _mod_spec(per_sample, d):
    if per_sample:
        return pl.BlockSpec((1, 1, d), lambda b, t: (b, 0, 0))
    return pl.BlockSpec((1, 1, d), lambda b, t: (0, 0, 0))


def _ab_in_kernel(x_ref, g_ref, sh_ref, sc_ref, w_ref, cs_ref, a_ref, y_ref, *, conv_ch, group_ch):
    u = _norm_modulate(x_ref[0], g_ref[...], sh_ref[0], sc_ref[0]).astype(BF16)
    z = _dot(u, w_ref[...])
    a_ref[0] = z[:, :conv_ch] * _sigmoid(z[:, conv_ch:2 * conv_ch])
    uf = z[:, 2 * conv_ch:].astype(BF16)
    for g in range(FOURIER_GROUPS):
        yg = _dot(uf[:, g * group_ch:(g + 1) * group_ch], cs_ref[...])
        y_ref[0, 0, :, g * group_ch:(g + 1) * group_ch] = yg[:, :group_ch].astype(BF16)
        y_ref[0, 1, :, g * group_ch:(g + 1) * group_ch] = yg[:, group_ch:].astype(BF16)


def _ab_in(x, g, sh, sc, per_sample, w_in, cs, conv_ch, tl):
    n, L, d = x.shape
    ab_in = w_in.shape[1]
    four_ch = ab_in - 2 * conv_ch
    group_ch = four_ch // FOURIER_GROUPS
    kern = functools.partial(_ab_in_kernel, conv_ch=conv_ch, group_ch=group_ch)
    return pl.pallas_call(
        kern,
        grid=(n, L // tl),
        in_specs=[
            pl.BlockSpec((1, tl, d), lambda b, t: (b, t, 0)),
            pl.BlockSpec((1, d), lambda b, t: (0, 0)),
            _mod_spec(per_sample, d),
            _mod_spec(per_sample, d),
            pl.BlockSpec((d, ab_in), lambda b, t: (0, 0)),
            pl.BlockSpec(cs.shape, lambda b, t: (0, 0)),
        ],
        out_specs=[
            pl.BlockSpec((1, tl, conv_ch), lambda b, t: (b, t, 0)),
            pl.BlockSpec((1, 2, tl, four_ch), lambda b, t: (b, 0, t, 0)),
        ],
        out_shape=[
            jax.ShapeDtypeStruct((n, L, conv_ch), F32),
            jax.ShapeDtypeStruct((n, 2, L, four_ch), BF16),
        ],
        compiler_params=_cparams("arbitrary", "arbitrary"),
        name="ab_in",
    )(x, g, sh, sc, w_in, cs)


def _ab_out_kernel(a_ref, y_ref, d_ref, x_ref, g1_ref, cw_ref, cb_ref, lg_ref, lb_ref, wo_ref,
                   o_ref, apad_ref, act_ref, *, L, tq, conv_ch):
    t = pl.program_id(1)

    @pl.when(t == 0)
    def _():
        apad_ref[0:CONV_HALO, :] = jnp.zeros((CONV_HALO, conv_ch), F32)
        apad_ref[CONV_HALO:CONV_HALO + L, :] = a_ref[0]
        apad_ref[CONV_HALO + L:CONV_HALO + L + CONV_HALO, :] = jnp.zeros((CONV_HALO, conv_ch), F32)

    t0 = pl.multiple_of(t * tq, tq)
    win = tq + 8 * CONV_TAP_GROUPS
    for c in range(conv_ch // LANES):
        cols = slice(c * LANES, (c + 1) * LANES)
        full = apad_ref[pl.ds(t0, win), cols]
        acc = jnp.zeros((tq, LANES), F32) + cb_ref[:, cols]
        for r in range(8):
            shifted = full if r == 0 else pltpu.roll(full, win - r, axis=0)
            for j in range(CONV_TAP_GROUPS):
                m = 8 * j + r
                if m == 0:
                    continue
                acc = acc + shifted[8 * j:8 * j + tq, :] * cw_ref[m:m + 1, cols]
        act_ref[:, cols] = acc
    four = _dot(d_ref[...], y_ref[0])
    conv = act_ref[...]
    mu = jnp.mean(conv, axis=-1, keepdims=True)
    cen = conv - mu
    var = jnp.mean(cen * cen, axis=-1, keepdims=True)
    ln = cen * lax.rsqrt(var + NORM_EPS) * lg_ref[...] + lb_ref[...]
    act = (ln * _sigmoid(ln)).astype(BF16)
    mix = _dot(act, wo_ref[0:conv_ch, :]) + _dot(four.astype(BF16), wo_ref[conv_ch:, :])
    o_ref[0] = x_ref[0] + g1_ref[0] * mix


def _ab_out(a, ystack, dmat, x, g1, per_sample, conv_w, conv_b, ln_g, ln_b, w_out, tq):
    n, L, d = x.shape
    conv_ch = a.shape[-1]
    four_ch = ystack.shape[-1]
    kern = functools.partial(_ab_out_kernel, L=L, tq=tq, conv_ch=conv_ch)
    vec = lambda b, t: (0, 0)
    return pl.pallas_call(
        kern,
        grid=(n, L // tq),
        in_specs=[
            pl.BlockSpec((1, L, conv_ch), lambda b, t: (b, 0, 0)),
            pl.BlockSpec((1, 2 * L, four_ch), lambda b, t: (b, 0, 0)),
            pl.BlockSpec((tq, 2 * L), lambda b, t: (t, 0)),
            pl.BlockSpec((1, tq, d), lambda b, t: (b, t, 0)),
            _mod_spec(per_sample, d),
            pl.BlockSpec(conv_w.shape, vec),
            pl.BlockSpec((1, conv_ch), vec),
            pl.BlockSpec((1, conv_ch), vec),
            pl.BlockSpec((1, conv_ch), vec),
            pl.BlockSpec(w_out.shape, vec),
        ],
        out_specs=pl.BlockSpec((1, tq, d), lambda b, t: (b, t, 0)),
        out_shape=jax.ShapeDtypeStruct((n, L, d), F32),
        scratch_shapes=[
            pltpu.VMEM((L + 2 * CONV_HALO, conv_ch), F32),
            pltpu.VMEM((tq, conv_ch), F32),
        ],
        compiler_params=_cparams("arbitrary", "arbitrary"),
        name="ab_out",
    )(a, ystack, dmat, x, g1, conv_w, conv_b, ln_g, ln_b, w_out)


def _dft_tables(L, group_ch):
    k = np.arange(L, dtype=np.int64)
    ang = 2.0 * np.pi * ((k[:, None] * k[None, :]) % L).astype(np.float64) / L
    dmat = np.concatenate([np.cos(ang), -np.sin(ang)], axis=1) / np.sqrt(L)
    m = np.arange(group_ch, dtype=np.int64)
    angc = 2.0 * np.pi * ((m[:, None] * m[None, :]) % group_ch).astype(np.float64) / group_ch
    cs = np.concatenate([np.cos(angc), np.sin(angc)], axis=1) / np.sqrt(group_ch)
    return jnp.asarray(dmat.astype(np.float32)).astype(BF16), jnp.asarray(cs.astype(np.float32)).astype(BF16)


def _conv_fourier_layer(x, norm_g, sh, sc, g1, per_sample, w_in, conv_w, conv_b, ln_g, ln_b, w_out, tl, tq):
    n, L, d = x.shape
    conv_ch = conv_w.shape[-1]
    four_ch = w_in.shape[1] - 2 * conv_ch
    dmat, cs = _dft_tables(L, four_ch // FOURIER_GROUPS)
    a, y = _ab_in(x, norm_g.reshape(1, d), sh, sc, per_sample, w_in.astype(BF16), cs, conv_ch, tl)
    cw = jnp.concatenate([jnp.zeros((1, conv_ch), F32), conv_w], axis=0)
    return _ab_out(a, y.reshape(n, 2 * L, four_ch), dmat, x, g1, per_sample, cw,
                   conv_b.reshape(1, conv_ch), ln_g.reshape(1, conv_ch), ln_b.reshape(1, conv_ch),
                   w_out.astype(BF16), tq)


def _rope_tables(L):
    rows = L // GRID_W
    row = np.repeat(np.arange(rows, dtype=np.float32), GRID_W)
    col = np.tile(np.arange(GRID_W, dtype=np.float32), rows)
    nf = QK_ROPE // 4
    inv_freq = (np.float32(ROPE_BASE) ** (-np.arange(nf, dtype=np.float32) / np.float32(nf))).astype(np.float32)
    ang = np.concatenate([row[:, None] * inv_freq, col[:, None] * inv_freq], axis=-1).astype(np.float32)
    cos = np.repeat(np.cos(ang.astype(np.float64)), 2, axis=-1)
    sin = np.repeat(np.sin(ang.astype(np.float64)), 2, axis=-1)
    cos_t = np.zeros((L, HEAD_PAD), np.float32)
    sin_t = np.zeros((L, HEAD_PAD), np.float32)
    cos_t[:, :QK_NOPE] = 1.0
    cos_t[:, QK_NOPE:QK_HEAD] = cos
    sin_t[:, QK_NOPE:QK_HEAD] = sin
    return jnp.asarray(cos_t), jnp.asarray(sin_t)


def _pair_swap(w):
    w2 = w.reshape(w.shape[:-1] + (QK_ROPE // 2, 2))
    return jnp.stack([-w2[..., 1], w2[..., 0]], axis=-1).reshape(w.shape)


def _mla_weights(w_in, w_uq, w_ukv):
    d = w_in.shape[0]
    h = MLA_HEADS
    zeros = lambda *s: jnp.zeros(s, F32)
    w_kr = w_in[:, Q_LORA + KV_LORA:]
    kr_a = jnp.concatenate([zeros(d, QK_NOPE), w_kr, zeros(d, HEAD_PAD - QK_HEAD)], axis=1)
    kr_b = jnp.concatenate([zeros(d, QK_NOPE), _pair_swap(w_kr), zeros(d, HEAD_PAD - QK_HEAD)], axis=1)
    w_in_ext = jnp.concatenate([w_in[:, :Q_LORA + KV_LORA], kr_a, kr_b], axis=1)
    wq = w_uq.reshape(Q_LORA, h, QK_HEAD)
    pad = zeros(Q_LORA, h, HEAD_PAD - QK_HEAD)
    wq_a = jnp.concatenate([wq, pad], axis=-1).reshape(Q_LORA, h * HEAD_PAD)
    wq_b = jnp.concatenate([zeros(Q_LORA, h, QK_NOPE), _pair_swap(wq[..., QK_NOPE:]), pad],
                           axis=-1).reshape(Q_LORA, h * HEAD_PAD)
    wkv = w_ukv.reshape(KV_LORA, h, QK_NOPE + V_HEAD)
    wk = jnp.concatenate([wkv[..., :QK_NOPE], zeros(KV_LORA, h, HEAD_PAD - QK_NOPE)],
                         axis=-1).reshape(KV_LORA, h * HEAD_PAD)
    wv = wkv[..., QK_NOPE:].reshape(KV_LORA, h * V_HEAD)
    return (w_in_ext.astype(BF16), jnp.concatenate([wq_a, wq_b], axis=1).astype(BF16),
            jnp.concatenate([wk, wv], axis=1).astype(BF16))


def _rms(x, g):
    return x * lax.rsqrt(jnp.mean(x * x, axis=-1, keepdims=True) + NORM_EPS) * g


def _mla_proj_kernel(*refs, latent):
    if latent:
        (x_ref, g_ref, sh_ref, sc_ref, win_ref, qg_ref, kvg_ref, wq_ref, wkv_ref, cos_ref, sin_ref,
         q_ref, k_ref, v_ref) = refs
    else:
        (x_ref, g_ref, sh_ref, sc_ref, win_ref, kvg_ref, wkv_ref, k_ref, v_ref) = refs
    hp = MLA_HEADS * HEAD_PAD
    u = _norm_modulate(x_ref[0], g_ref[...], sh_ref[0], sc_ref[0]).astype(BF16)
    z = _dot(u, win_ref[...])
    ckv = _rms(z[:, Q_LORA:Q_LORA + KV_LORA], kvg_ref[...]).astype(BF16)
    kr = z[:, Q_LORA + KV_LORA:Q_LORA + KV_LORA + HEAD_PAD]
    if latent:
        cos = cos_ref[...]
        sin = sin_ref[...]
        kr = kr * cos + z[:, Q_LORA + KV_LORA + HEAD_PAD:] * sin
        cq = _rms(z[:, :Q_LORA], qg_ref[...]).astype(BF16)
    pair_w = 2 * HEAD_PAD
    for hpair in range(MLA_HEADS // 2):
        pcols = slice(hpair * pair_w, (hpair + 1) * pair_w)
        if latent:
            qa = _dot(cq, wq_ref[:, pcols])
            qb = _dot(cq, wq_ref[:, hp + hpair * pair_w:hp + (hpair + 1) * pair_w])
        kn = _dot(ckv, wkv_ref[:, pcols])
        for par in range(2):
            h = 2 * hpair + par
            cols = slice(par * HEAD_PAD, (par + 1) * HEAD_PAD)
            if latent:
                q_ref[0, h] = ((qa[:, cols] * cos + qb[:, cols] * sin) * SCORE_SCALE).T.astype(BF16)
            k_ref[0, h] = (kn[:, cols] + kr).astype(BF16)
    vv = _dot(ckv, wkv_ref[:, hp:])
    for hpair in range(MLA_HEADS // 2):
        v_ref[0, hpair] = vv[:, hpair * 2 * V_HEAD:(hpair + 1) * 2 * V_HEAD].T.astype(BF16)


def _mla_proj(x, g, sh, sc, per_sample, w_in_ext, q_g, kv_g, wq, wkv, cos_t, sin_t, latent, tl):
    n, L, d = x.shape
    vec = lambda b, t: (0, 0)
    row = lambda b, t: (b, t, 0)
    in_specs = [
        pl.BlockSpec((1, tl, d), row),
        pl.BlockSpec((1, d), vec),
        _mod_spec(per_sample, d),
        _mod_spec(per_sample, d),
        pl.BlockSpec(w_in_ext.shape, vec),
    ]
    args = [x, g, sh, sc, w_in_ext]
    if latent:
        in_specs += [pl.BlockSpec((1, Q_LORA), vec)]
        args += [q_g]
    in_specs += [pl.BlockSpec((1, KV_LORA), vec)]
    args += [kv_g]
    if latent:
        in_specs += [pl.BlockSpec(wq.shape, vec)]
        args += [wq]
    in_specs += [pl.BlockSpec(wkv.shape, vec)]
    args += [wkv]
    nh, nhp = MLA_HEADS, MLA_HEADS // 2
    head_row = lambda b, t: (b, 0, t, 0)
    head_col = lambda b, t: (b, 0, 0, t)
    out_specs = [pl.BlockSpec((1, nh, tl, HEAD_PAD), head_row), pl.BlockSpec((1, nhp, V_ROWS, tl), head_col)]
    out_shape = [jax.ShapeDtypeStruct((n, nh, L, HEAD_PAD), BF16),
                 jax.ShapeDtypeStruct((n, nhp, V_ROWS, L), BF16)]
    if latent:
        in_specs += [pl.BlockSpec((tl, HEAD_PAD), lambda b, t: (t, 0))] * 2
        args += [cos_t, sin_t]
        out_specs = [pl.BlockSpec((1, nh, HEAD_PAD, tl), head_col)] + out_specs
        out_shape = [jax.ShapeDtypeStruct((n, nh, HEAD_PAD, L), BF16)] + out_shape
    return pl.pallas_call(
        functools.partial(_mla_proj_kernel, latent=latent),
        grid=(n, L // tl),
        in_specs=in_specs,
        out_specs=out_specs,
        out_shape=out_shape,
        compiler_params=_cparams("arbitrary", "arbitrary"),
        name="mla_proj_lat" if latent else "mla_proj_ctx",
    )(*args)


def _mla_attn_kernel(q_ref, kc_ref, kl_ref, vc_ref, vl_ref, x_ref, g1_ref, wo_ref, o_ref, oh_ref, s_ref,
                     *, n_ctx, n_lat):
    tq = q_ref.shape[3]
    chunks = [(True, 0, n_ctx, 0)]
    chunks += [(False, k0, KEY_CHUNK, n_ctx + k0) for k0 in range(0, n_lat, KEY_CHUNK)]
    even_rows = lax.broadcasted_iota(jnp.int32, (2 * V_HEAD, tq), 0) < V_HEAD

    def scores(h, par):
        qt = q_ref[0, h]
        mx = jnp.full((1, tq), -jnp.inf, F32)
        for is_ctx, k0, kn, r0 in chunks:
            k_ref = kc_ref if is_ctx else kl_ref
            s = _dot(k_ref[0, h, k0:k0 + kn, :], qt)
            s_ref[par, r0:r0 + kn, :] = s
            mx = jnp.maximum(mx, jnp.max(s, axis=0, keepdims=True))
        return mx

    def values(hpair, par, mx):
        den = jnp.zeros((1, tq), F32)
        acc = jnp.zeros((V_ROWS, tq), F32)
        for is_ctx, k0, kn, r0 in chunks:
            v_ref = vc_ref if is_ctx else vl_ref
            p = jnp.exp2(s_ref[par, r0:r0 + kn, :] - mx)
            den = den + jnp.sum(p, axis=0, keepdims=True)
            acc = acc + _dot(v_ref[0, hpair, :, k0:k0 + kn], p.astype(BF16))
        return acc * (1.0 / den)

    def head_pair(hpair, mx_even, last):
        mx_odd = scores(2 * hpair + 1, 1)
        out_even = values(hpair, 0, mx_even)
        mx_next = mx_even if last else scores(2 * hpair + 2, 0)
        out_odd = values(hpair, 1, mx_odd)
        oh_ref[hpair] = jnp.where(even_rows, out_even, out_odd).T.astype(BF16)
        return mx_next

    n_pairs = MLA_HEADS // 2
    mx0 = lax.fori_loop(0, n_pairs - 1, lambda i, mx: head_pair(i, mx, False), scores(0, 0))
    head_pair(n_pairs - 1, mx0, True)
    heads = jnp.concatenate([oh_ref[i] for i in range(n_pairs)], axis=1)
    o_ref[0] = x_ref[0] + g1_ref[0] * _dot(heads, wo_ref[...])


def _mla_attn(q, k_ctx, k_lat, v_ctx, v_lat, x, g1, w_o, tq):
    n, L, d = x.shape
    n_ctx = k_ctx.shape[2]
    nh, nhp = MLA_HEADS, MLA_HEADS // 2
    samp = lambda b, t: (b, 0, 0, 0)
    row = lambda b, t: (b, t, 0)
    assert L % KEY_CHUNK == 0
    return pl.pallas_call(
        functools.partial(_mla_attn_kernel, n_ctx=n_ctx, n_lat=L),
        grid=(n, L // tq),
        in_specs=[
            pl.BlockSpec((1, nh, HEAD_PAD, tq), lambda b, t: (b, 0, 0, t)),
            pl.BlockSpec((1, nh, n_ctx, HEAD_PAD), samp),
            pl.BlockSpec((1, nh, L, HEAD_PAD), samp),
            pl.BlockSpec((1, nhp, V_ROWS, n_ctx), samp),
            pl.BlockSpec((1, nhp, V_ROWS, L), samp, pipeline_mode=pl.Buffered(1)),
            pl.BlockSpec((1, tq, d), row),
            _mod_spec(True, d),
            pl.BlockSpec(w_o.shape, lambda b, t: (0, 0), pipeline_mode=pl.Buffered(1)),
        ],
        out_specs=pl.BlockSpec((1, tq, d), row),
        out_shape=jax.ShapeDtypeStruct((n, L, d), F32),
        scratch_shapes=[pltpu.VMEM((nhp, tq, 2 * V_HEAD), BF16),
                        pltpu.VMEM((2, n_ctx + L, tq), F32)],
        compiler_params=_cparams("arbitrary", "arbitrary"),
        name="mla_attn",
    )(q, k_ctx, k_lat, v_ctx, v_lat, x, g1, w_o)


def _route_kernel(x_ref, g_ref, sh_ref, sc_ref, wr_ref, u_ref, slot_ref, aff_ref, slott_ref, first_ref,
                  afft_ref, tri_ref, *, L, tl, cap, gs):
    s = pl.program_id(1)
    t = pl.program_id(2)
    ne = N_EXPERTS
    rows = gs * ne

    @pl.when((pl.program_id(0) == 0) & (s == 0) & (t == 0))
    def _():
        r = lax.broadcasted_iota(jnp.int32, (PREFIX_BLOCK, PREFIX_BLOCK), 0)
        c = lax.broadcasted_iota(jnp.int32, (PREFIX_BLOCK, PREFIX_BLOCK), 1)
        tri_ref[...] = jnp.where(r < c, 1.0, 0.0).astype(BF16)

    u = _norm_modulate(x_ref[0], g_ref[...], sh_ref[0], sc_ref[0]).astype(BF16)
    u_ref[0] = u
    logits = _dot_nt(wr_ref[...], u)
    e = jnp.exp(logits - jnp.max(logits, axis=0, keepdims=True))
    t0 = pl.multiple_of(t * tl, tl)
    afft_ref[pl.ds(pl.multiple_of(s * ne, ne), ne), pl.ds(t0, tl)] = e / jnp.sum(e, axis=0, keepdims=True)

    @pl.when((s == gs - 1) & (t == pl.num_programs(2) - 1))
    def _():
        aff = afft_ref[...]

        def as_f32(bits):
            return lax.bitcast_convert_type(bits, F32)

        def count_ge(th):
            return jnp.sum(jnp.where(aff >= th, 1.0, 0.0), axis=1, keepdims=True)

        def body(_, carry):
            lo, hi = carry
            mid = lo + ((hi - lo) >> 1)
            ok = count_ge(as_f32(mid)) >= cap
            return jnp.where(ok, mid, lo), jnp.where(ok, hi, mid)

        lo0 = jnp.zeros((rows, 1), jnp.int32)
        hi0 = jnp.full((rows, 1), 0x7F800000, jnp.int32)
        lo_bits, hi_bits = lax.fori_loop(0, 31, body, (lo0, hi0))
        gt = aff >= as_f32(hi_bits)
        eq = (aff >= as_f32(lo_bits)) & jnp.logical_not(gt)
        need = cap - jnp.sum(jnp.where(gt, 1.0, 0.0), axis=1, keepdims=True)

        def prefix_count(mask):
            ones = jnp.where(mask, 1.0, 0.0)
            base = jnp.zeros((rows, 1), F32)
            parts = []
            for j in range(L // PREFIX_BLOCK):
                blk = ones[:, j * PREFIX_BLOCK:(j + 1) * PREFIX_BLOCK]
                parts.append(_dot(blk.astype(BF16), tri_ref[...]) + base)
                base = base + jnp.sum(blk, axis=1, keepdims=True)
            return parts[0] if len(parts) == 1 else jnp.concatenate(parts, axis=1)

        sel = gt | (eq & (prefix_count(eq) < need))
        pos = prefix_count(sel)
        tok = lax.broadcasted_iota(jnp.int32, (L, LANES), 0)
        blk_start = lax.broadcasted_iota(jnp.int32, (L, LANES), 1) * PREFIX_BLOCK
        before = jnp.where(tok < blk_start, 1.0, 0.0).astype(BF16)
        first_i = _dot(jnp.where(sel, 1.0, 0.0).astype(BF16), before).astype(jnp.int32)
        slot = jnp.where(sel, pos, -1.0)
        slot_i = slot.astype(jnp.int32)
        fill = jnp.full((LANES - ne, L), -1.0, F32)
        for si in range(gs):
            first_ref[si] = first_i[si * ne:(si + 1) * ne]
            for ex in range(ne):
                r = si * ne + ex
                slot_ref[si, ex] = slot_i[r:r + 1, :]
                aff_ref[si, ex] = aff[r:r + 1, :]
            slott_ref[si] = jnp.concatenate([slot[si * ne:(si + 1) * ne], fill], axis=0).T


def _route(x, g, sh, sc, per_sample, w_router_t, cap, tl, gs):
    n, L, d = x.shape
    ne = N_EXPERTS
    assert n % gs == 0 and L % tl == 0
    vec = lambda gi, s, t: (0, 0)
    row = lambda gi, s, t: (gi * gs + s, t, 0)
    group4 = lambda gi, s, t: (gi, 0, 0, 0)
    if per_sample:
        mod_spec = pl.BlockSpec((1, 1, d), lambda gi, s, t: (gi * gs + s, 0, 0))
    else:
        mod_spec = pl.BlockSpec((1, 1, d), lambda gi, s, t: (0, 0, 0))
    return pl.pallas_call(
        functools.partial(_route_kernel, L=L, tl=tl, cap=cap, gs=gs),
        grid=(n // gs, gs, L // tl),
        in_specs=[
            pl.BlockSpec((1, tl, d), row),
            pl.BlockSpec((1, d), vec),
            mod_spec,
            mod_spec,
            pl.BlockSpec((ne, d), vec),
        ],
        out_specs=[
            pl.BlockSpec((1, tl, d), row),
            pl.BlockSpec((gs, ne, 1, L), group4),
            pl.BlockSpec((gs, ne, 1, L), group4),
            pl.BlockSpec((gs, L, LANES), lambda gi, s, t: (gi, 0, 0)),
            pl.BlockSpec((gs, ne, LANES), lambda gi, s, t: (gi, 0, 0)),
        ],
        out_shape=[
            jax.ShapeDtypeStruct((n, L, d), BF16),
            jax.ShapeDtypeStruct((n, ne, 1, L), jnp.int32),
            jax.ShapeDtypeStruct((n, ne, 1, L), F32),
            jax.ShapeDtypeStruct((n, L, LANES), F32),
            jax.ShapeDtypeStruct((n, ne, LANES), jnp.int32),
        ],
        scratch_shapes=[pltpu.VMEM((gs * ne, L), F32), pltpu.VMEM((PREFIX_BLOCK, PREFIX_BLOCK), BF16)],
        compiler_params=_cparams("arbitrary", "arbitrary", "arbitrary"),
        name="route",
    )(x, g, sh, sc, w_router_t)


def _expert_kernel(first_ref, u_ref, slot_ref, aff_ref, w1_ref, w3_ref, w2_ref, y_ref, w1b, w3b, w2b,
                   xg_ref, gate_ref, *, nb, cap):
    e = pl.program_id(0)
    bb = pl.program_id(1)

    @pl.when(bb == 0)
    def _():
        w1b[...] = w1_ref[0, 0].astype(BF16)
        w3b[...] = w3_ref[0, 0].astype(BF16)
        w2b[...] = w2_ref[0, 0].astype(BF16)

    L = u_ref.shape[1]
    nblk = L // PREFIX_BLOCK

    def gather_dense():
        slot_iota = lax.broadcasted_iota(jnp.int32, (cap, L), 0)
        for s in range(nb):
            hit = slot_ref[s, 0] == slot_iota
            xg_ref[s * cap:(s + 1) * cap, :] = _dot(jnp.where(hit, 1.0, 0.0).astype(BF16), u_ref[s])
            gate_ref[s * cap:(s + 1) * cap, :] = jnp.broadcast_to(
                jnp.sum(jnp.where(hit, aff_ref[s, 0], 0.0), axis=1, keepdims=True), (cap, LANES))

    if cap <= GATHER_ROWS:
        gather_dense()
    else:
        starts, fits = [], None
        for s in range(nb):
            base = ((e * pl.num_programs(1) + bb) * nb + s) * FIRST_STRIDE
            for j in range(nblk):
                start = (first_ref[base + j] // SUBLANES) * SUBLANES
                ok = first_ref[base + j + 1] - start <= GATHER_ROWS
                starts.append(start)
                fits = ok if fits is None else jnp.logical_and(fits, ok)

        @pl.when(fits)
        def _():
            xg_ref[...] = jnp.zeros(xg_ref.shape, F32)
            gate_ref[...] = jnp.zeros(gate_ref.shape, F32)
            row_iota = lax.broadcasted_iota(jnp.int32, (GATHER_ROWS, PREFIX_BLOCK), 0)
            for s in range(nb):
                for j in range(nblk):
                    toks = slice(j * PREFIX_BLOCK, (j + 1) * PREFIX_BLOCK)
                    start = pl.multiple_of(starts[s * nblk + j], SUBLANES)
                    hit = slot_ref[s, 0, :, toks] == (row_iota + start)
                    rows = pl.ds(s * cap + start, GATHER_ROWS)
                    xg_ref[rows, :] += _dot(jnp.where(hit, 1.0, 0.0).astype(BF16), u_ref[s, toks, :])
                    gate_ref[rows, :] += jnp.broadcast_to(
                        jnp.sum(jnp.where(hit, aff_ref[s, 0, :, toks], 0.0), axis=1, keepdims=True),
                        (GATHER_ROWS, LANES))

        @pl.when(jnp.logical_not(fits))
        def _():
            gather_dense()

    xg = xg_ref[0:nb * cap, :].astype(BF16)
    gate = gate_ref[0:nb * cap, 0:1]
    h1 = _dot(xg, w1b[...])
    hid = (h1 * _sigmoid(h1) * _dot(xg, w3b[...])).astype(BF16)
    y = (_dot(hid, w2b[...]) * gate).astype(BF16)
    for s in range(nb):
        y_ref[0, s] = y[s * cap:(s + 1) * cap]


def _experts(u, slot, aff, first, layer, w1, w3, w2, cap, nb):
    n, L, d = u.shape
    _, ne, _, ff = w1.shape
    assert L // PREFIX_BLOCK < FIRST_STRIDE
    first_flat = jnp.transpose(first[:, :, :FIRST_STRIDE], (1, 0, 2)).reshape(-1)
    wspec = lambda shape: pl.BlockSpec((1, 1) + shape, lambda e, b, tab: (layer, e, 0, 0))
    rows = nb * cap + GATHER_ROWS
    grid_spec = pltpu.PrefetchScalarGridSpec(
        num_scalar_prefetch=1,
        grid=(ne, n // nb),
        in_specs=[
            pl.BlockSpec((nb, L, d), lambda e, b, tab: (b, 0, 0)),
            pl.BlockSpec((nb, 1, 1, L), lambda e, b, tab: (b, e, 0, 0)),
            pl.BlockSpec((nb, 1, 1, L), lambda e, b, tab: (b, e, 0, 0)),
            wspec((d, ff)), wspec((d, ff)), wspec((ff, d)),
        ],
        out_specs=pl.BlockSpec((1, nb, cap, d), lambda e, b, tab: (e, b, 0, 0)),
        scratch_shapes=[pltpu.VMEM((d, ff), BF16), pltpu.VMEM((d, ff), BF16), pltpu.VMEM((ff, d), BF16),
                        pltpu.VMEM((rows, d), F32), pltpu.VMEM((rows, LANES), F32)],
    )
    return pl.pallas_call(
        functools.partial(_expert_kernel, nb=nb, cap=cap),
        grid_spec=grid_spec,
        out_shape=jax.ShapeDtypeStruct((ne, n, cap, d), BF16),
        compiler_params=_cparams("arbitrary", "arbitrary"),
        name="experts",
    )(first_flat, u, slot, aff, w1, w3, w2)


def _combine_kernel(*refs, cap, final):
    if final:
        y_ref, st_ref, x_ref, g2_ref, fg_ref, o_ref = refs
    else:
        y_ref, st_ref, x_ref, g2_ref, o_ref = refs
    tl = x_ref.shape[1]
    d = x_ref.shape[2]
    st = st_ref[0]
    lane = lax.broadcasted_iota(jnp.int32, (tl, cap), 1).astype(F32)
    onehot = jnp.concatenate(
        [jnp.where(st[:, e:e + 1] == lane, 1.0, 0.0).astype(BF16) for e in range(N_EXPERTS)], axis=1)
    moe = _dot(onehot, y_ref[...].reshape(N_EXPERTS * cap, d))
    out = x_ref[0] + g2_ref[0] * moe
    if final:
        out = _rms(out, fg_ref[...])
    o_ref[0] = out


def _combine(y, slot_t, x, g2, per_sample, cap, tl, final_g=None):
    n, L, d = x.shape
    ne = N_EXPERTS
    final = final_g is not None
    in_specs = [
        pl.BlockSpec((ne, 1, cap, d), lambda b, t: (0, b, 0, 0)),
        pl.BlockSpec((1, tl, LANES), lambda b, t: (b, t, 0)),
        pl.BlockSpec((1, tl, d), lambda b, t: (b, t, 0)),
        _mod_spec(per_sample, d),
    ]
    args = [y, slot_t, x, g2]
    if final:
        in_specs += [pl.BlockSpec((1, d), lambda b, t: (0, 0))]
        args += [final_g.reshape(1, d)]
    return pl.pallas_call(
        functools.partial(_combine_kernel, cap=cap, final=final),
        grid=(n, L // tl),
        in_specs=in_specs,
        out_specs=pl.BlockSpec((1, tl, d), lambda b, t: (b, t, 0)),
        out_shape=jax.ShapeDtypeStruct((n, L, d), F32),
        compiler_params=_cparams("arbitrary", "arbitrary"),
        name="combine",
    )(*args)


def _ec_moe_layer(x, norm_g, sh, sc, g2, per_sample, layer, w_router, w1, w3, w2, tl, nb, gs, final_g=None):
    n, L, d = x.shape
    cap = EC_CAPACITY_FACTOR * L // N_EXPERTS
    u, slot, aff, slot_t, first = _route(x, norm_g.reshape(1, d), sh, sc, per_sample,
                                         w_router.T.astype(BF16), cap, tl, gs)
    y = _experts(u, slot, aff, first, layer, w1, w3, w2, cap, nb)
    return _combine(y, slot_t, x, g2, per_sample, cap, tl, final_g)


def kernel(x, c, ctx, c_ctx, mod_w, mod_b, norm1_g, norm2_g, ab_w_in, ab_conv_w, ab_conv_b, ab_ln_g, ab_ln_b, ab_w_out, mla_w_in, mla_q_norm_g, mla_kv_norm_g, mla_w_uq, mla_w_ukv, mla_w_o, moe_w_router, moe_w1, moe_w3, moe_w2, final_g):
    n, L, d = x.shape
    n_ctx = ctx.shape[1]
    depth = mod_w.shape[0]
    assert n < MOD_ROWS
    cvecs = jnp.concatenate([c, c_ctx[None, :], jnp.zeros((MOD_ROWS - n - 1, d), F32)], axis=0)
    mod = _modulation(cvecs, mod_w, mod_b)

    def mods(i, ctx_stream):
        rows = mod[i, n:n + 1] if ctx_stream else mod[i, :n]
        return [rows[:, k * d:(k + 1) * d].reshape(-1, 1, d) for k in range(N_MOD)]

    x_lat, x_ctx = x, ctx
    for i in range(depth):
        ctx_out = i < depth - 1
        last = i == depth - 1
        j = i // 2
        sh1, sc1, g1, sh2, sc2, g2 = mods(i, False)
        csh1, csc1, cg1, csh2, csc2, cg2 = mods(i, True)
        if i % 2 == 0:
            ab = (ab_w_in[j], ab_conv_w[j], ab_conv_b[j], ab_ln_g[j], ab_ln_b[j], ab_w_out[j])
            x_lat = _conv_fourier_layer(x_lat, norm1_g[i], sh1, sc1, g1, True, *ab, tl=512, tq=256)
            if ctx_out:
                x_ctx = _conv_fourier_layer(x_ctx, norm1_g[i], csh1, csc1, cg1, False, *ab,
                                            tl=n_ctx, tq=n_ctx)
        else:
            assert not ctx_out, "context-stream attention output is only needed for deeper stacks"
            w_in_ext, wq, wkv = _mla_weights(mla_w_in[j], mla_w_uq[j], mla_w_ukv[j])
            cos_t, sin_t = _rope_tables(L)
            g = norm1_g[i].reshape(1, d)
            q_g = mla_q_norm_g[j].reshape(1, Q_LORA)
            kv_g = mla_kv_norm_g[j].reshape(1, KV_LORA)
            q, k_lat, v_lat = _mla_proj(x_lat, g, sh1, sc1, True, w_in_ext, q_g, kv_g, wq, wkv,
                                        cos_t, sin_t, True, 512)
            k_ctx, v_ctx = _mla_proj(x_ctx, g, csh1, csc1, False,
                                     w_in_ext[:, :Q_LORA + KV_LORA + HEAD_PAD], None, kv_g, None, wkv,
                                     None, None, False, n_ctx)
            x_lat = _mla_attn(q, k_ctx, k_lat, v_ctx, v_lat, x_lat, g1, mla_w_o[j].astype(BF16), 512)
        moe = (i, moe_w_router[i], moe_w1, moe_w3, moe_w2)
        x_lat = _ec_moe_layer(x_lat, norm2_g[i], sh2, sc2, g2, True, *moe, tl=512, nb=2, gs=4,
                              final_g=final_g if last else None)
        if ctx_out:
            x_ctx = _ec_moe_layer(x_ctx, norm2_g[i], csh2, csc2, cg2, False, *moe, tl=n_ctx, nb=n, gs=n)
    return x_lat
```

```python
import functools

import numpy as np
import jax
import jax.numpy as jnp
from jax import lax
from jax.experimental import pallas as pl
from jax.experimental.pallas import tpu as pltpu

F32 = jnp.float32
BF16 = jnp.bfloat16

NORM_EPS = 1e-6
N_MOD = 6
GRID_W = 64
CONV_WIDTH = 31
CONV_HALO = 16
CONV_TAP_GROUPS = 4
FOURIER_GROUPS = 4
MLA_HEADS = 16
QK_NOPE = 64
QK_ROPE = 32
V_HEAD = 64
V_ROWS = 2 * V_HEAD
QK_HEAD = QK_NOPE + QK_ROPE
HEAD_PAD = 128
Q_LORA = 256
KV_LORA = 128
ROPE_BASE = 10000.0
SCORE_SCALE = float(QK_HEAD ** -0.5 * np.log2(np.e))
KEY_CHUNK = 512
N_EXPERTS = 16
PREFIX_BLOCK = 256
GATHER_ROWS = 128
SUBLANES = 8
FIRST_BLOCK = 128
FIRST_STRIDE = 32
SCATTER_ROWS = 128
BF16_ROWS = 16
EC_CAPACITY_FACTOR = 2
MOD_ROWS = 24
LANES = 128
VMEM_LIMIT = 56 * 1024 * 1024


def _cparams(*sem):
    return pltpu.CompilerParams(dimension_semantics=sem, vmem_limit_bytes=VMEM_LIMIT)


def _dot(a, b):
    return jnp.dot(a, b, preferred_element_type=F32)


def _dot_nt(a, b):
    return lax.dot_general(a, b, (((1,), (1,)), ((), ())), preferred_element_type=F32)


def _sigmoid(x):
    return 1.0 / (1.0 + jnp.exp(-x))


def _norm_modulate(x, g, shift, scale):
    ms = jnp.mean(x * x, axis=-1, keepdims=True)
    y = x * lax.rsqrt(ms + NORM_EPS) * g
    return y * (1.0 + scale) + shift


def _mod_kernel(c_ref, w_ref, b_ref, o_ref):
    c = c_ref[...]
    s = (c * _sigmoid(c)).astype(BF16)
    o_ref[0] = _dot(s, w_ref[0].astype(BF16)) + b_ref[0]


def _modulation(cvecs, mod_w, mod_b):
    depth, d, nd = mod_w.shape
    tn = 1024
    return pl.pallas_call(
        _mod_kernel,
        grid=(depth, nd // tn),
        in_specs=[
            pl.BlockSpec((MOD_ROWS, d), lambda i, j: (0, 0)),
            pl.BlockSpec((1, d, tn), lambda i, j: (i, 0, j)),
            pl.BlockSpec((1, 1, tn), lambda i, j: (i, 0, j)),
        ],
        out_specs=pl.BlockSpec((1, MOD_ROWS, tn), lambda i, j: (i, 0, j)),
        out_shape=jax.ShapeDtypeStruct((depth, MOD_ROWS, nd), F32),
        compiler_params=_cparams("arbitrary", "arbitrary"),
        name="modulation",
    )(cvecs, mod_w, mod_b.reshape(depth, 1, nd))


def _mod_spec(per_sample, d):
    if per_sample:
        return pl.BlockSpec((1, 1, d), lambda b, t: (b, 0, 0))
    return pl.BlockSpec((1, 1, d), lambda b, t: (0, 0, 0))


def _ab_in_kernel(x_ref, g_ref, sh_ref, sc_ref, w_ref, cs_ref, a_ref, y_ref, *, conv_ch, group_ch):
    u = _norm_modulate(x_ref[0], g_ref[...], sh_ref[0], sc_ref[0]).astype(BF16)
    z = _dot(u, w_ref[...])
    a_ref[0] = z[:, :conv_ch] * _sigmoid(z[:, conv_ch:2 * conv_ch])
    uf = z[:, 2 * conv_ch:].astype(BF16)
    for g in range(FOURIER_GROUPS):
        yg = _dot(uf[:, g * group_ch:(g + 1) * group_ch], cs_ref[...])
        y_ref[0, 0, :, g * group_ch:(g + 1) * group_ch] = yg[:, :group_ch].astype(BF16)
        y_ref[0, 1, :, g * group_ch:(g + 1) * group_ch] = yg[:, group_ch:].astype(BF16)


def _ab_in(x, g, sh, sc, per_sample, w_in, cs, conv_ch, tl):
    n, L, d = x.shape
    ab_in = w_in.shape[1]
    four_ch = ab_in - 2 * conv_ch
    group_ch = four_ch // FOURIER_GROUPS
    kern = functools.partial(_ab_in_kernel, conv_ch=conv_ch, group_ch=group_ch)
    return pl.pallas_call(
        kern,
        grid=(n, L // tl),
        in_specs=[
            pl.BlockSpec((1, tl, d), lambda b, t: (b, t, 0)),
            pl.BlockSpec((1, d), lambda b, t: (0, 0)),
            _mod_spec(per_sample, d),
            _mod_spec(per_sample, d),
            pl.BlockSpec((d, ab_in), lambda b, t: (0, 0)),
            pl.BlockSpec(cs.shape, lambda b, t: (0, 0)),
        ],
        out_specs=[
            pl.BlockSpec((1, tl, conv_ch), lambda b, t: (b, t, 0)),
            pl.BlockSpec((1, 2, tl, four_ch), lambda b, t: (b, 0, t, 0)),
        ],
        out_shape=[
            jax.ShapeDtypeStruct((n, L, conv_ch), F32),
            jax.ShapeDtypeStruct((n, 2, L, four_ch), BF16),
        ],
        compiler_params=_cparams("arbitrary", "arbitrary"),
        name="ab_in",
    )(x, g, sh, sc, w_in, cs)


def _ab_out_kernel(a_ref, y_ref, d_ref, x_ref, g1_ref, cw_ref, cb_ref, lg_ref, lb_ref, wo_ref,
                   o_ref, apad_ref, act_ref, *, L, tq, conv_ch):
    t = pl.program_id(1)

    @pl.when(t == 0)
    def _():
        apad_ref[0:CONV_HALO, :] = jnp.zeros((CONV_HALO, conv_ch), F32)
        apad_ref[CONV_HALO:CONV_HALO + L, :] = a_ref[0]
        apad_ref[CONV_HALO + L:CONV_HALO + L + CONV_HALO, :] = jnp.zeros((CONV_HALO, conv_ch), F32)

    t0 = pl.multiple_of(t * tq, tq)
    win = tq + 8 * CONV_TAP_GROUPS
    for c in range(conv_ch // LANES):
        cols = slice(c * LANES, (c + 1) * LANES)
        full = apad_ref[pl.ds(t0, win), cols]
        acc = jnp.zeros((tq, LANES), F32) + cb_ref[:, cols]
        for r in range(8):
            shifted = full if r == 0 else pltpu.roll(full, win - r, axis=0)
            for j in range(CONV_TAP_GROUPS):
                m = 8 * j + r
                if m == 0:
                    continue
                acc = acc + shifted[8 * j:8 * j + tq, :] * cw_ref[m:m + 1, cols]
        act_ref[:, cols] = acc
    four = _dot(d_ref[...], y_ref[0])
    conv = act_ref[...]
    mu = jnp.mean(conv, axis=-1, keepdims=True)
    cen = conv - mu
    var = jnp.mean(cen * cen, axis=-1, keepdims=True)
    ln = cen * lax.rsqrt(var + NORM_EPS) * lg_ref[...] + lb_ref[...]
    act = (ln * _sigmoid(ln)).astype(BF16)
    mix = _dot(act, wo_ref[0:conv_ch, :]) + _dot(four.astype(BF16), wo_ref[conv_ch:, :])
    o_ref[0] = x_ref[0] + g1_ref[0] * mix


def _ab_out(a, ystack, dmat, x, g1, per_sample, conv_w, conv_b, ln_g, ln_b, w_out, tq):
    n, L, d = x.shape
    conv_ch = a.shape[-1]
    four_ch = ystack.shape[-1]
    kern = functools.partial(_ab_out_kernel, L=L, tq=tq, conv_ch=conv_ch)
    vec = lambda b, t: (0, 0)
    return pl.pallas_call(
        kern,
        grid=(n, L // tq),
        in_specs=[
            pl.BlockSpec((1, L, conv_ch), lambda b, t: (b, 0, 0)),
            pl.BlockSpec((1, 2 * L, four_ch), lambda b, t: (b, 0, 0)),
            pl.BlockSpec((tq, 2 * L), lambda b, t: (t, 0)),
            pl.BlockSpec((1, tq, d), lambda b, t: (b, t, 0)),
            _mod_spec(per_sample, d),
            pl.BlockSpec(conv_w.shape, vec),
            pl.BlockSpec((1, conv_ch), vec),
            pl.BlockSpec((1, conv_ch), vec),
            pl.BlockSpec((1, conv_ch), vec),
            pl.BlockSpec(w_out.shape, vec),
        ],
        out_specs=pl.BlockSpec((1, tq, d), lambda b, t: (b, t, 0)),
        out_shape=jax.ShapeDtypeStruct((n, L, d), F32),
        scratch_shapes=[
            pltpu.VMEM((L + 2 * CONV_HALO, conv_ch), F32),
            pltpu.VMEM((tq, conv_ch), F32),
        ],
        compiler_params=_cparams("arbitrary", "arbitrary"),
        name="ab_out",
    )(a, ystack, dmat, x, g1, conv_w, conv_b, ln_g, ln_b, w_out)


def _dft_tables(L, group_ch):
    k = np.arange(L, dtype=np.int64)
    ang = 2.0 * np.pi * ((k[:, None] * k[None, :]) % L).astype(np.float64) / L
    dmat = np.concatenate([np.cos(ang), -np.sin(ang)], axis=1) / np.sqrt(L)
    m = np.arange(group_ch, dtype=np.int64)
    angc = 2.0 * np.pi * ((m[:, None] * m[None, :]) % group_ch).astype(np.float64) / group_ch
    cs = np.concatenate([np.cos(angc), np.sin(angc)], axis=1) / np.sqrt(group_ch)
    return jnp.asarray(dmat.astype(np.float32)).astype(BF16), jnp.asarray(cs.astype(np.float32)).astype(BF16)


def _conv_fourier_layer(x, norm_g, sh, sc, g1, per_sample, w_in, conv_w, conv_b, ln_g, ln_b, w_out, tl, tq):
    n, L, d = x.shape
    conv_ch = conv_w.shape[-1]
    four_ch = w_in.shape[1] - 2 * conv_ch
    dmat, cs = _dft_tables(L, four_ch // FOURIER_GROUPS)
    a, y = _ab_in(x, norm_g.reshape(1, d), sh, sc, per_sample, w_in.astype(BF16), cs, conv_ch, tl)
    cw = jnp.concatenate([jnp.zeros((1, conv_ch), F32), conv_w], axis=0)
    return _ab_out(a, y.reshape(n, 2 * L, four_ch), dmat, x, g1, per_sample, cw,
                   conv_b.reshape(1, conv_ch), ln_g.reshape(1, conv_ch), ln_b.reshape(1, conv_ch),
                   w_out.astype(BF16), tq)


def _rope_tables(L):
    rows = L // GRID_W
    row = np.repeat(np.arange(rows, dtype=np.float32), GRID_W)
    col = np.tile(np.arange(GRID_W, dtype=np.float32), rows)
    nf = QK_ROPE // 4
    inv_freq = (np.float32(ROPE_BASE) ** (-np.arange(nf, dtype=np.float32) / np.float32(nf))).astype(np.float32)
    ang = np.concatenate([row[:, None] * inv_freq, col[:, None] * inv_freq], axis=-1).astype(np.float32)
    cos = np.repeat(np.cos(ang.astype(np.float64)), 2, axis=-1)
    sin = np.repeat(np.sin(ang.astype(np.float64)), 2, axis=-1)
    cos_t = np.zeros((L, HEAD_PAD), np.float32)
    sin_t = np.zeros((L, HEAD_PAD), np.float32)
    cos_t[:, :QK_NOPE] = 1.0
    cos_t[:, QK_NOPE:QK_HEAD] = cos
    sin_t[:, QK_NOPE:QK_HEAD] = sin
    return jnp.asarray(cos_t), jnp.asarray(sin_t)


def _pair_swap(w):
    w2 = w.reshape(w.shape[:-1] + (QK_ROPE // 2, 2))
    return jnp.stack([-w2[..., 1], w2[..., 0]], axis=-1).reshape(w.shape)


def _mla_weights(w_in, w_uq, w_ukv):
    d = w_in.shape[0]
    h = MLA_HEADS
    zeros = lambda *s: jnp.zeros(s, F32)
    w_kr = w_in[:, Q_LORA + KV_LORA:]
    kr_a = jnp.concatenate([zeros(d, QK_NOPE), w_kr, zeros(d, HEAD_PAD - QK_HEAD)], axis=1)
    kr_b = jnp.concatenate([zeros(d, QK_NOPE), _pair_swap(w_kr), zeros(d, HEAD_PAD - QK_HEAD)], axis=1)
    w_in_ext = jnp.concatenate([w_in[:, :Q_LORA + KV_LORA], kr_a, kr_b], axis=1)
    wq = w_uq.reshape(Q_LORA, h, QK_HEAD)
    pad = zeros(Q_LORA, h, HEAD_PAD - QK_HEAD)
    wq_a = jnp.concatenate([wq, pad], axis=-1).reshape(Q_LORA, h * HEAD_PAD)
    wq_b = jnp.concatenate([zeros(Q_LORA, h, QK_NOPE), _pair_swap(wq[..., QK_NOPE:]), pad],
                           axis=-1).reshape(Q_LORA, h * HEAD_PAD)
    wkv = w_ukv.reshape(KV_LORA, h, QK_NOPE + V_HEAD)
    wk = jnp.concatenate([wkv[..., :QK_NOPE], zeros(KV_LORA, h, HEAD_PAD - QK_NOPE)],
                         axis=-1).reshape(KV_LORA, h * HEAD_PAD)
    wv = wkv[..., QK_NOPE:].reshape(KV_LORA, h * V_HEAD)
    return (w_in_ext.astype(BF16), jnp.concatenate([wq_a, wq_b], axis=1).astype(BF16),
            jnp.concatenate([wk, wv], axis=1).astype(BF16))


def _rms(x, g):
    return x * lax.rsqrt(jnp.mean(x * x, axis=-1, keepdims=True) + NORM_EPS) * g


def _mla_proj_kernel(*refs, latent):
    if latent:
        (x_ref, g_ref, sh_ref, sc_ref, win_ref, qg_ref, kvg_ref, wq_ref, wkv_ref, cos_ref, sin_ref,
         q_ref, k_ref, v_ref) = refs
    else:
        (x_ref, g_ref, sh_ref, sc_ref, win_ref, kvg_ref, wkv_ref, k_ref, v_ref) = refs
    hp = MLA_HEADS * HEAD_PAD
    u = _norm_modulate(x_ref[0], g_ref[...], sh_ref[0], sc_ref[0]).astype(BF16)
    z = _dot(u, win_ref[...])
    ckv = _rms(z[:, Q_LORA:Q_LORA + KV_LORA], kvg_ref[...]).astype(BF16)
    kr = z[:, Q_LORA + KV_LORA:Q_LORA + KV_LORA + HEAD_PAD]
    if latent:
        cos = cos_ref[...]
        sin = sin_ref[...]
        kr = kr * cos + z[:, Q_LORA + KV_LORA + HEAD_PAD:] * sin
        cq = _rms(z[:, :Q_LORA], qg_ref[...]).astype(BF16)
    pair_w = 2 * HEAD_PAD
    for hpair in range(MLA_HEADS // 2):
        pcols = slice(hpair * pair_w, (hpair + 1) * pair_w)
        if latent:
            qa = _dot(cq, wq_ref[:, pcols])
            qb = _dot(cq, wq_ref[:, hp + hpair * pair_w:hp + (hpair + 1) * pair_w])
        kn = _dot(ckv, wkv_ref[:, pcols])
        for par in range(2):
            h = 2 * hpair + par
            cols = slice(par * HEAD_PAD, (par + 1) * HEAD_PAD)
            if latent:
                q_ref[0, h] = ((qa[:, cols] * cos + qb[:, cols] * sin) * SCORE_SCALE).T.astype(BF16)
            k_ref[0, h] = (kn[:, cols] + kr).astype(BF16)
    vv = _dot(ckv, wkv_ref[:, hp:])
    for hpair in range(MLA_HEADS // 2):
        v_ref[0, hpair] = vv[:, hpair * 2 * V_HEAD:(hpair + 1) * 2 * V_HEAD].T.astype(BF16)


def _mla_proj(x, g, sh, sc, per_sample, w_in_ext, q_g, kv_g, wq, wkv, cos_t, sin_t, latent, tl):
    n, L, d = x.shape
    vec = lambda b, t: (0, 0)
    row = lambda b, t: (b, t, 0)
    in_specs = [
        pl.BlockSpec((1, tl, d), row),
        pl.BlockSpec((1, d), vec),
        _mod_spec(per_sample, d),
        _mod_spec(per_sample, d),
        pl.BlockSpec(w_in_ext.shape, vec),
    ]
    args = [x, g, sh, sc, w_in_ext]
    if latent:
        in_specs += [pl.BlockSpec((1, Q_LORA), vec)]
        args += [q_g]
    in_specs += [pl.BlockSpec((1, KV_LORA), vec)]
    args += [kv_g]
    if latent:
        in_specs += [pl.BlockSpec(wq.shape, vec)]
        args += [wq]
    in_specs += [pl.BlockSpec(wkv.shape, vec)]
    args += [wkv]
    nh, nhp = MLA_HEADS, MLA_HEADS // 2
    head_row = lambda b, t: (b, 0, t, 0)
    head_col = lambda b, t: (b, 0, 0, t)
    out_specs = [pl.BlockSpec((1, nh, tl, HEAD_PAD), head_row), pl.BlockSpec((1, nhp, V_ROWS, tl), head_col)]
    out_shape = [jax.ShapeDtypeStruct((n, nh, L, HEAD_PAD), BF16),
                 jax.ShapeDtypeStruct((n, nhp, V_ROWS, L), BF16)]
    if latent:
        in_specs += [pl.BlockSpec((tl, HEAD_PAD), lambda b, t: (t, 0))] * 2
        args += [cos_t, sin_t]
        out_specs = [pl.BlockSpec((1, nh, HEAD_PAD, tl), head_col)] + out_specs
        out_shape = [jax.ShapeDtypeStruct((n, nh, HEAD_PAD, L), BF16)] + out_shape
    return pl.pallas_call(
        functools.partial(_mla_proj_kernel, latent=latent),
        grid=(n, L // tl),
        in_specs=in_specs,
        out_specs=out_specs,
        out_shape=out_shape,
        compiler_params=_cparams("arbitrary", "arbitrary"),
        name="mla_proj_lat" if latent else "mla_proj_ctx",
    )(*args)


def _mla_attn_kernel(q_ref, kc_ref, kl_ref, vc_ref, vl_ref, x_ref, g1_ref, wo_ref, o_ref, oh_ref, s_ref,
                     *, n_ctx, n_lat):
    tq = q_ref.shape[3]
    chunks = [(True, 0, n_ctx, 0)]
    chunks += [(False, k0, KEY_CHUNK, n_ctx + k0) for k0 in range(0, n_lat, KEY_CHUNK)]
    even_rows = lax.broadcasted_iota(jnp.int32, (2 * V_HEAD, tq), 0) < V_HEAD

    def scores(h, par):
        qt = q_ref[0, h]
        mx = jnp.full((1, tq), -jnp.inf, F32)
        for is_ctx, k0, kn, r0 in chunks:
            k_ref = kc_ref if is_ctx else kl_ref
            s = _dot(k_ref[0, h, k0:k0 + kn, :], qt)
            s_ref[par, r0:r0 + kn, :] = s
            mx = jnp.maximum(mx, jnp.max(s, axis=0, keepdims=True))
        return mx

    def values(hpair, par, mx):
        den = jnp.zeros((1, tq), F32)
        acc = jnp.zeros((V_ROWS, tq), F32)
        for is_ctx, k0, kn, r0 in chunks:
            v_ref = vc_ref if is_ctx else vl_ref
            p = jnp.exp2(s_ref[par, r0:r0 + kn, :] - mx)
            den = den + jnp.sum(p, axis=0, keepdims=True)
            acc = acc + _dot(v_ref[0, hpair, :, k0:k0 + kn], p.astype(BF16))
        return acc * (1.0 / den)

    def head_pair(hpair, mx_even, last):
        mx_odd = scores(2 * hpair + 1, 1)
        out_even = values(hpair, 0, mx_even)
        mx_next = mx_even if last else scores(2 * hpair + 2, 0)
        out_odd = values(hpair, 1, mx_odd)
        oh_ref[hpair] = jnp.where(even_rows, out_even, out_odd).T.astype(BF16)
        return mx_next

    n_pairs = MLA_HEADS // 2
    mx0 = lax.fori_loop(0, n_pairs - 1, lambda i, mx: head_pair(i, mx, False), scores(0, 0))
    head_pair(n_pairs - 1, mx0, True)
    heads = jnp.concatenate([oh_ref[i] for i in range(n_pairs)], axis=1)
    o_ref[0] = x_ref[0] + g1_ref[0] * _dot(heads, wo_ref[...])


def _mla_attn(q, k_ctx, k_lat, v_ctx, v_lat, x, g1, w_o, tq):
    n, L, d = x.shape
    n_ctx = k_ctx.shape[2]
    nh, nhp = MLA_HEADS, MLA_HEADS // 2
    samp = lambda b, t: (b, 0, 0, 0)
    row = lambda b, t: (b, t, 0)
    assert L % KEY_CHUNK == 0
    return pl.pallas_call(
        functools.partial(_mla_attn_kernel, n_ctx=n_ctx, n_lat=L),
        grid=(n, L // tq),
        in_specs=[
            pl.BlockSpec((1, nh, HEAD_PAD, tq), lambda b, t: (b, 0, 0, t)),
            pl.BlockSpec((1, nh, n_ctx, HEAD_PAD), samp),
            pl.BlockSpec((1, nh, L, HEAD_PAD), samp),
            pl.BlockSpec((1, nhp, V_ROWS, n_ctx), samp),
            pl.BlockSpec((1, nhp, V_ROWS, L), samp, pipeline_mode=pl.Buffered(1)),
            pl.BlockSpec((1, tq, d), row),
            _mod_spec(True, d),
            pl.BlockSpec(w_o.shape, lambda b, t: (0, 0), pipeline_mode=pl.Buffered(1)),
        ],
        out_specs=pl.BlockSpec((1, tq, d), row),
        out_shape=jax.ShapeDtypeStruct((n, L, d), F32),
        scratch_shapes=[pltpu.VMEM((nhp, tq, 2 * V_HEAD), BF16),
                        pltpu.VMEM((2, n_ctx + L, tq), F32)],
        compiler_params=_cparams("arbitrary", "arbitrary"),
        name="mla_attn",
    )(q, k_ctx, k_lat, v_ctx, v_lat, x, g1, w_o)


def _route_kernel(x_ref, g_ref, sh_ref, sc_ref, wr_ref, u_ref, slot_ref, aff_ref, slott_ref, first_ref,
                  afft_ref, tri_ref, *, L, tl, cap, gs):
    s = pl.program_id(1)
    t = pl.program_id(2)
    ne = N_EXPERTS
    rows = gs * ne

    @pl.when((pl.program_id(0) == 0) & (s == 0) & (t == 0))
    def _():
        r = lax.broadcasted_iota(jnp.int32, (PREFIX_BLOCK, PREFIX_BLOCK), 0)
        c = lax.broadcasted_iota(jnp.int32, (PREFIX_BLOCK, PREFIX_BLOCK), 1)
        tri_ref[...] = jnp.where(r < c, 1.0, 0.0).astype(BF16)

    u = _norm_modulate(x_ref[0], g_ref[...], sh_ref[0], sc_ref[0]).astype(BF16)
    u_ref[0] = u
    logits = _dot_nt(wr_ref[...], u)
    e = jnp.exp(logits - jnp.max(logits, axis=0, keepdims=True))
    t0 = pl.multiple_of(t * tl, tl)
    afft_ref[pl.ds(pl.multiple_of(s * ne, ne), ne), pl.ds(t0, tl)] = e / jnp.sum(e, axis=0, keepdims=True)

    @pl.when((s == gs - 1) & (t == pl.num_programs(2) - 1))
    def _():
        aff = afft_ref[...]

        def as_f32(bits):
            return lax.bitcast_convert_type(bits, F32)

        def count_ge(th):
            return jnp.sum(jnp.where(aff >= th, 1.0, 0.0), axis=1, keepdims=True)

        def body(_, carry):
            lo, hi = carry
            mid = lo + ((hi - lo) >> 1)
            ok = count_ge(as_f32(mid)) >= cap
            return jnp.where(ok, mid, lo), jnp.where(ok, hi, mid)

        lo0 = jnp.zeros((rows, 1), jnp.int32)
        hi0 = jnp.full((rows, 1), 0x7F800000, jnp.int32)
        lo_bits, hi_bits = lax.fori_loop(0, 31, body, (lo0, hi0))
        gt = aff >= as_f32(hi_bits)
        eq = (aff >= as_f32(lo_bits)) & jnp.logical_not(gt)
        need = cap - jnp.sum(jnp.where(gt, 1.0, 0.0), axis=1, keepdims=True)

        def prefix_count(mask):
            ones = jnp.where(mask, 1.0, 0.0)
            base = jnp.zeros((rows, 1), F32)
            parts = []
            for j in range(L // PREFIX_BLOCK):
                blk = ones[:, j * PREFIX_BLOCK:(j + 1) * PREFIX_BLOCK]
                parts.append(_dot(blk.astype(BF16), tri_ref[...]) + base)
                base = base + jnp.sum(blk, axis=1, keepdims=True)
            return parts[0] if len(parts) == 1 else jnp.concatenate(parts, axis=1)

        sel = gt | (eq & (prefix_count(eq) < need))
        pos = prefix_count(sel)
        tok = lax.broadcasted_iota(jnp.int32, (L, LANES), 0)
        blk_start = lax.broadcasted_iota(jnp.int32, (L, LANES), 1) * FIRST_BLOCK
        before = jnp.where(tok < blk_start, 1.0, 0.0).astype(BF16)
        first_i = _dot(jnp.where(sel, 1.0, 0.0).astype(BF16), before).astype(jnp.int32)
        slot = jnp.where(sel, pos, -1.0)
        slot_i = slot.astype(jnp.int32)
        fill = jnp.full((LANES - ne, L), -1.0, F32)
        for si in range(gs):
            first_ref[si] = first_i[si * ne:(si + 1) * ne]
            for ex in range(ne):
                r = si * ne + ex
                slot_ref[si, ex] = slot_i[r:r + 1, :]
                aff_ref[si, ex] = aff[r:r + 1, :]
            slott_ref[si] = jnp.concatenate([slot[si * ne:(si + 1) * ne], fill], axis=0).T


def _route(x, g, sh, sc, per_sample, w_router_t, cap, tl, gs):
    n, L, d = x.shape
    ne = N_EXPERTS
    assert n % gs == 0 and L % tl == 0
    vec = lambda gi, s, t: (0, 0)
    row = lambda gi, s, t: (gi * gs + s, t, 0)
    group4 = lambda gi, s, t: (gi, 0, 0, 0)
    if per_sample:
        mod_spec = pl.BlockSpec((1, 1, d), lambda gi, s, t: (gi * gs + s, 0, 0))
    else:
        mod_spec = pl.BlockSpec((1, 1, d), lambda gi, s, t: (0, 0, 0))
    return pl.pallas_call(
        functools.partial(_route_kernel, L=L, tl=tl, cap=cap, gs=gs),
        grid=(n // gs, gs, L // tl),
        in_specs=[
            pl.BlockSpec((1, tl, d), row),
            pl.BlockSpec((1, d), vec),
            mod_spec,
            mod_spec,
            pl.BlockSpec((ne, d), vec),
        ],
        out_specs=[
            pl.BlockSpec((1, tl, d), row),
            pl.BlockSpec((gs, ne, 1, L), group4),
            pl.BlockSpec((gs, ne, 1, L), group4),
            pl.BlockSpec((gs, L, LANES), lambda gi, s, t: (gi, 0, 0)),
            pl.BlockSpec((gs, ne, LANES), lambda gi, s, t: (gi, 0, 0)),
        ],
        out_shape=[
            jax.ShapeDtypeStruct((n, L, d), BF16),
            jax.ShapeDtypeStruct((n, ne, 1, L), jnp.int32),
            jax.ShapeDtypeStruct((n, ne, 1, L), F32),
            jax.ShapeDtypeStruct((n, L, LANES), F32),
            jax.ShapeDtypeStruct((n, ne, LANES), jnp.int32),
        ],
        scratch_shapes=[pltpu.VMEM((gs * ne, L), F32), pltpu.VMEM((PREFIX_BLOCK, PREFIX_BLOCK), BF16)],
        compiler_params=_cparams("arbitrary", "arbitrary", "arbitrary"),
        name="route",
    )(x, g, sh, sc, w_router_t)


def _expert_kernel(first_ref, u_ref, slot_ref, aff_ref, w1_ref, w3_ref, w2_ref, y_ref, w1b, w3b, w2b,
                   xg_ref, gate_ref, *, nb, cap):
    e = pl.program_id(0)
    bb = pl.program_id(1)

    @pl.when(bb == 0)
    def _():
        w1b[...] = w1_ref[0, 0].astype(BF16)
        w3b[...] = w3_ref[0, 0].astype(BF16)
        w2b[...] = w2_ref[0, 0].astype(BF16)

    L = u_ref.shape[1]
    nblk = L // PREFIX_BLOCK

    def gather_dense():
        slot_iota = lax.broadcasted_iota(jnp.int32, (cap, L), 0)
        for s in range(nb):
            hit = slot_ref[s, 0] == slot_iota
            xg_ref[s * cap:(s + 1) * cap, :] = _dot(jnp.where(hit, 1.0, 0.0).astype(BF16), u_ref[s])
            gate_ref[s * cap:(s + 1) * cap, :] = jnp.broadcast_to(
                jnp.sum(jnp.where(hit, aff_ref[s, 0], 0.0), axis=1, keepdims=True), (cap, LANES))

    if cap <= GATHER_ROWS:
        gather_dense()
    else:
        starts, fits = [], None
        for s in range(nb):
            base = ((e * pl.num_programs(1) + bb) * nb + s) * FIRST_STRIDE
            per = PREFIX_BLOCK // FIRST_BLOCK
            for j in range(nblk):
                start = (first_ref[base + per * j] // SUBLANES) * SUBLANES
                ok = first_ref[base + per * (j + 1)] - start <= GATHER_ROWS
                starts.append(start)
                fits = ok if fits is None else jnp.logical_and(fits, ok)

        @pl.when(fits)
        def _():
            xg_ref[...] = jnp.zeros(xg_ref.shape, F32)
            gate_ref[...] = jnp.zeros(gate_ref.shape, F32)
            row_iota = lax.broadcasted_iota(jnp.int32, (GATHER_ROWS, PREFIX_BLOCK), 0)
            for s in range(nb):
                for j in range(nblk):
                    toks = slice(j * PREFIX_BLOCK, (j + 1) * PREFIX_BLOCK)
                    start = pl.multiple_of(starts[s * nblk + j], SUBLANES)
                    hit = slot_ref[s, 0, :, toks] == (row_iota + start)
                    rows = pl.ds(s * cap + start, GATHER_ROWS)
                    xg_ref[rows, :] += _dot(jnp.where(hit, 1.0, 0.0).astype(BF16), u_ref[s, toks, :])
                    gate_ref[rows, :] += jnp.broadcast_to(
                        jnp.sum(jnp.where(hit, aff_ref[s, 0, :, toks], 0.0), axis=1, keepdims=True),
                        (GATHER_ROWS, LANES))

        @pl.when(jnp.logical_not(fits))
        def _():
            gather_dense()

    xg = xg_ref[0:nb * cap, :].astype(BF16)
    gate = gate_ref[0:nb * cap, 0:1]
    h1 = _dot(xg, w1b[...])
    hid = (h1 * _sigmoid(h1) * _dot(xg, w3b[...])).astype(BF16)
    y = (_dot(hid, w2b[...]) * gate).astype(BF16)
    for s in range(nb):
        y_ref[0, s] = y[s * cap:(s + 1) * cap]


def _experts(u, slot, aff, first, layer, w1, w3, w2, cap, nb):
    n, L, d = u.shape
    _, ne, _, ff = w1.shape
    assert L // FIRST_BLOCK < FIRST_STRIDE
    first_flat = jnp.transpose(first[:, :, :FIRST_STRIDE], (1, 0, 2)).reshape(-1)
    wspec = lambda shape: pl.BlockSpec((1, 1) + shape, lambda e, b, tab: (layer, e, 0, 0))
    rows = nb * cap + GATHER_ROWS
    grid_spec = pltpu.PrefetchScalarGridSpec(
        num_scalar_prefetch=1,
        grid=(ne, n // nb),
        in_specs=[
            pl.BlockSpec((nb, L, d), lambda e, b, tab: (b, 0, 0)),
            pl.BlockSpec((nb, 1, 1, L), lambda e, b, tab: (b, e, 0, 0)),
            pl.BlockSpec((nb, 1, 1, L), lambda e, b, tab: (b, e, 0, 0)),
            wspec((d, ff)), wspec((d, ff)), wspec((ff, d)),
        ],
        out_specs=pl.BlockSpec((1, nb, cap, d), lambda e, b, tab: (e, b, 0, 0)),
        scratch_shapes=[pltpu.VMEM((d, ff), BF16), pltpu.VMEM((d, ff), BF16), pltpu.VMEM((ff, d), BF16),
                        pltpu.VMEM((rows, d), F32), pltpu.VMEM((rows, LANES), F32)],
    )
    return pl.pallas_call(
        functools.partial(_expert_kernel, nb=nb, cap=cap),
        grid_spec=grid_spec,
        out_shape=jax.ShapeDtypeStruct((ne, n, cap, d), BF16),
        compiler_params=_cparams("arbitrary", "arbitrary"),
        name="experts",
    )(first_flat, u, slot, aff, w1, w3, w2)


def _combine_kernel(*refs, cap, final):
    if final:
        first_ref, y_ref, st_ref, x_ref, g2_ref, fg_ref, o_ref = refs
    else:
        first_ref, y_ref, st_ref, x_ref, g2_ref, o_ref = refs
    b = pl.program_id(0)
    t = pl.program_id(1)
    tl = x_ref.shape[1]
    d = x_ref.shape[2]
    ne = N_EXPERTS

    def finish(rows, moe):
        out = x_ref[0, rows, :] + g2_ref[0] * moe
        if final:
            out = _rms(out, fg_ref[...])
        o_ref[0, rows, :] = out

    def scatter_dense():
        st = st_ref[0]
        lane = lax.broadcasted_iota(jnp.int32, (tl, cap), 1).astype(F32)
        onehot = jnp.concatenate(
            [jnp.where(st[:, e:e + 1] == lane, 1.0, 0.0).astype(BF16) for e in range(ne)], axis=1)
        finish(slice(0, tl), _dot(onehot, y_ref[...].reshape(ne * cap, d)))

    if cap <= SCATTER_ROWS:
        scatter_dense()
        return

    nsub = tl // FIRST_BLOCK
    starts, fits = [], None
    for q in range(nsub):
        for e in range(ne):
            entry = (b * ne + e) * FIRST_STRIDE + t * nsub + q
            start = jnp.minimum((first_ref[entry] // BF16_ROWS) * BF16_ROWS, cap - SCATTER_ROWS)
            ok = first_ref[entry + 1] - start <= SCATTER_ROWS
            starts.append(start)
            fits = ok if fits is None else jnp.logical_and(fits, ok)

    @pl.when(fits)
    def _():
        lane = lax.broadcasted_iota(jnp.int32, (FIRST_BLOCK, SCATTER_ROWS), 1).astype(F32)
        for q in range(nsub):
            rows = slice(q * FIRST_BLOCK, (q + 1) * FIRST_BLOCK)
            st = st_ref[0, rows, :]
            hots, wins = [], []
            for e in range(ne):
                start = pl.multiple_of(starts[q * ne + e], BF16_ROWS)
                hots.append(jnp.where(st[:, e:e + 1] - start.astype(F32) == lane, 1.0, 0.0).astype(BF16))
                wins.append(y_ref[e, 0, pl.ds(start, SCATTER_ROWS), :])
            finish(rows, _dot(jnp.concatenate(hots, axis=1), jnp.concatenate(wins, axis=0)))

    @pl.when(jnp.logical_not(fits))
    def _():
        scatter_dense()


def _combine(y, slot_t, first, x, g2, per_sample, cap, tl, final_g=None):
    n, L, d = x.shape
    ne = N_EXPERTS
    final = final_g is not None
    assert L // FIRST_BLOCK < FIRST_STRIDE and tl % FIRST_BLOCK == 0
    first_flat = first[:, :, :FIRST_STRIDE].reshape(-1)
    if per_sample:
        mod_spec = pl.BlockSpec((1, 1, d), lambda b, t, tab: (b, 0, 0))
    else:
        mod_spec = pl.BlockSpec((1, 1, d), lambda b, t, tab: (0, 0, 0))
    in_specs = [
        pl.BlockSpec((ne, 1, cap, d), lambda b, t, tab: (0, b, 0, 0)),
        pl.BlockSpec((1, tl, LANES), lambda b, t, tab: (b, t, 0)),
        pl.BlockSpec((1, tl, d), lambda b, t, tab: (b, t, 0)),
        mod_spec,
    ]
    args = [first_flat, y, slot_t, x, g2]
    if final:
        in_specs += [pl.BlockSpec((1, d), lambda b, t, tab: (0, 0))]
        args += [final_g.reshape(1, d)]
    grid_spec = pltpu.PrefetchScalarGridSpec(
        num_scalar_prefetch=1,
        grid=(n, L // tl),
        in_specs=in_specs,
        out_specs=pl.BlockSpec((1, tl, d), lambda b, t, tab: (b, t, 0)),
    )
    return pl.pallas_call(
        functools.partial(_combine_kernel, cap=cap, final=final),
        grid_spec=grid_spec,
        out_shape=jax.ShapeDtypeStruct((n, L, d), F32),
        compiler_params=_cparams("arbitrary", "arbitrary"),
        name="combine",
    )(*args)


def _ec_moe_layer(x, norm_g, sh, sc, g2, per_sample, layer, w_router, w1, w3, w2, tl, nb, gs, final_g=None):
    n, L, d = x.shape
    cap = EC_CAPACITY_FACTOR * L // N_EXPERTS
    u, slot, aff, slot_t, first = _route(x, norm_g.reshape(1, d), sh, sc, per_sample,
                                         w_router.T.astype(BF16), cap, tl, gs)
    y = _experts(u, slot, aff, first, layer, w1, w3, w2, cap, nb)
    return _combine(y, slot_t, first, x, g2, per_sample, cap, tl, final_g)


def kernel(x, c, ctx, c_ctx, mod_w, mod_b, norm1_g, norm2_g, ab_w_in, ab_conv_w, ab_conv_b, ab_ln_g, ab_ln_b, ab_w_out, mla_w_in, mla_q_norm_g, mla_kv_norm_g, mla_w_uq, mla_w_ukv, mla_w_o, moe_w_router, moe_w1, moe_w3, moe_w2, final_g):
    n, L, d = x.shape
    n_ctx = ctx.shape[1]
    depth = mod_w.shape[0]
    assert n < MOD_ROWS
    cvecs = jnp.concatenate([c, c_ctx[None, :], jnp.zeros((MOD_ROWS - n - 1, d), F32)], axis=0)
    mod = _modulation(cvecs, mod_w, mod_b)

    def mods(i, ctx_stream):
        rows = mod[i, n:n + 1] if ctx_stream else mod[i, :n]
        return [rows[:, k * d:(k + 1) * d].reshape(-1, 1, d) for k in range(N_MOD)]

    x_lat, x_ctx = x, ctx
    for i in range(depth):
        ctx_out = i < depth - 1
        last = i == depth - 1
        j = i // 2
        sh1, sc1, g1, sh2, sc2, g2 = mods(i, False)
        csh1, csc1, cg1, csh2, csc2, cg2 = mods(i, True)
        if i % 2 == 0:
            ab = (ab_w_in[j], ab_conv_w[j], ab_conv_b[j], ab_ln_g[j], ab_ln_b[j], ab_w_out[j])
            x_lat = _conv_fourier_layer(x_lat, norm1_g[i], sh1, sc1, g1, True, *ab, tl=512, tq=256)
            if ctx_out:
                x_ctx = _conv_fourier_layer(x_ctx, norm1_g[i], csh1, csc1, cg1, False, *ab,
                                            tl=n_ctx, tq=n_ctx)
        else:
            assert not ctx_out, "context-stream attention output is only needed for deeper stacks"
            w_in_ext, wq, wkv = _mla_weights(mla_w_in[j], mla_w_uq[j], mla_w_ukv[j])
            cos_t, sin_t = _rope_tables(L)
            g = norm1_g[i].reshape(1, d)
            q_g = mla_q_norm_g[j].reshape(1, Q_LORA)
            kv_g = mla_kv_norm_g[j].reshape(1, KV_LORA)
            q, k_lat, v_lat = _mla_proj(x_lat, g, sh1, sc1, True, w_in_ext, q_g, kv_g, wq, wkv,
                                        cos_t, sin_t, True, 512)
            k_ctx, v_ctx = _mla_proj(x_ctx, g, csh1, csc1, False,
                                     w_in_ext[:, :Q_LORA + KV_LORA + HEAD_PAD], None, kv_g, None, wkv,
                                     None, None, False, n_ctx)
            x_lat = _mla_attn(q, k_ctx, k_lat, v_ctx, v_lat, x_lat, g1, mla_w_o[j].astype(BF16), 512)
        moe = (i, moe_w_router[i], moe_w1, moe_w3, moe_w2)
        x_lat = _ec_moe_layer(x_lat, norm2_g[i], sh2, sc2, g2, True, *moe, tl=512, nb=2, gs=4,
                              final_g=final_g if last else None)
        if ctx_out:
            x_ctx = _ec_moe_layer(x_ctx, norm2_g[i], csh2, csc2, cg2, False, *moe, tl=n_ctx, nb=n, gs=n)
    return x_lat
```

```python
import functools

import numpy as np
import jax
import jax.numpy as jnp
from jax import lax
from jax.experimental import pallas as pl
from jax.experimental.pallas import tpu as pltpu

F32 = jnp.float32
BF16 = jnp.bfloat16

NORM_EPS = 1e-6
N_MOD = 6
GRID_W = 64
CONV_WIDTH = 31
CONV_HALO = 16
CONV_TAP_GROUPS = 4
FOURIER_GROUPS = 4
MLA_HEADS = 16
QK_NOPE = 64
QK_ROPE = 32
V_HEAD = 64
V_ROWS = 2 * V_HEAD
QK_HEAD = QK_NOPE + QK_ROPE
HEAD_PAD = 128
Q_LORA = 256
KV_LORA = 128
ROPE_BASE = 10000.0
SCORE_SCALE = float(QK_HEAD ** -0.5 * np.log2(np.e))
KEY_CHUNK = 512
N_EXPERTS = 16
PREFIX_BLOCK = 256
GATHER_ROWS = 128
SUBLANES = 8
FIRST_BLOCK = 128
FIRST_STRIDE = 32
SCATTER_ROWS = 64
BF16_ROWS = 16
EC_CAPACITY_FACTOR = 2
MOD_ROWS = 24
LANES = 128
VMEM_LIMIT = 56 * 1024 * 1024


def _cparams(*sem):
    return pltpu.CompilerParams(dimension_semantics=sem, vmem_limit_bytes=VMEM_LIMIT)


def _dot(a, b):
    return jnp.dot(a, b, preferred_element_type=F32)


def _dot_nt(a, b):
    return lax.dot_general(a, b, (((1,), (1,)), ((), ())), preferred_element_type=F32)


def _sigmoid(x):
    return 1.0 / (1.0 + jnp.exp(-x))


def _norm_modulate(x, g, shift, scale):
    ms = jnp.mean(x * x, axis=-1, keepdims=True)
    y = x * lax.rsqrt(ms + NORM_EPS) * g
    return y * (1.0 + scale) + shift


def _mod_kernel(c_ref, w_ref, b_ref, o_ref):
    c = c_ref[...]
    s = (c * _sigmoid(c)).astype(BF16)
    o_ref[0] = _dot(s, w_ref[0].astype(BF16)) + b_ref[0]


def _modulation(cvecs, mod_w, mod_b):
    depth, d, nd = mod_w.shape
    tn = 1024
    return pl.pallas_call(
        _mod_kernel,
        grid=(depth, nd // tn),
        in_specs=[
            pl.BlockSpec((MOD_ROWS, d), lambda i, j: (0, 0)),
            pl.BlockSpec((1, d, tn), lambda i, j: (i, 0, j)),
            pl.BlockSpec((1, 1, tn), lambda i, j: (i, 0, j)),
        ],
        out_specs=pl.BlockSpec((1, MOD_ROWS, tn), lambda i, j: (i, 0, j)),
        out_shape=jax.ShapeDtypeStruct((depth, MOD_ROWS, nd), F32),
        compiler_params=_cparams("arbitrary", "arbitrary"),
        name="modulation",
    )(cvecs, mod_w, mod_b.reshape(depth, 1, nd))


def _mod_spec(per_sample, d):
    if per_sample:
        return pl.BlockSpec((1, 1, d), lambda b, t: (b, 0, 0))
    return pl.BlockSpec((1, 1, d), lambda b, t: (0, 0, 0))


def _ab_in_kernel(x_ref, g_ref, sh_ref, sc_ref, w_ref, cs_ref, a_ref, y_ref, *, conv_ch, group_ch):
    u = _norm_modulate(x_ref[0], g_ref[...], sh_ref[0], sc_ref[0]).astype(BF16)
    z = _dot(u, w_ref[...])
    a_ref[0] = z[:, :conv_ch] * _sigmoid(z[:, conv_ch:2 * conv_ch])
    uf = z[:, 2 * conv_ch:].astype(BF16)
    for g in range(FOURIER_GROUPS):
        yg = _dot(uf[:, g * group_ch:(g + 1) * group_ch], cs_ref[...])
        y_ref[0, 0, :, g * group_ch:(g + 1) * group_ch] = yg[:, :group_ch].astype(BF16)
        y_ref[0, 1, :, g * group_ch:(g + 1) * group_ch] = yg[:, group_ch:].astype(BF16)


def _ab_in(x, g, sh, sc, per_sample, w_in, cs, conv_ch, tl):
    n, L, d = x.shape
    ab_in = w_in.shape[1]
    four_ch = ab_in - 2 * conv_ch
    group_ch = four_ch // FOURIER_GROUPS
    kern = functools.partial(_ab_in_kernel, conv_ch=conv_ch, group_ch=group_ch)
    return pl.pallas_call(
        kern,
        grid=(n, L // tl),
        in_specs=[
            pl.BlockSpec((1, tl, d), lambda b, t: (b, t, 0)),
            pl.BlockSpec((1, d), lambda b, t: (0, 0)),
            _mod_spec(per_sample, d),
            _mod_spec(per_sample, d),
            pl.BlockSpec((d, ab_in), lambda b, t: (0, 0)),
            pl.BlockSpec(cs.shape, lambda b, t: (0, 0)),
        ],
        out_specs=[
            pl.BlockSpec((1, tl, conv_ch), lambda b, t: (b, t, 0)),
            pl.BlockSpec((1, 2, tl, four_ch), lambda b, t: (b, 0, t, 0)),
        ],
        out_shape=[
            jax.ShapeDtypeStruct((n, L, conv_ch), F32),
            jax.ShapeDtypeStruct((n, 2, L, four_ch), BF16),
        ],
        compiler_params=_cparams("arbitrary", "arbitrary"),
        name="ab_in",
    )(x, g, sh, sc, w_in, cs)


def _ab_out_kernel(a_ref, y_ref, d_ref, x_ref, g1_ref, cw_ref, cb_ref, lg_ref, lb_ref, wo_ref,
                   o_ref, apad_ref, act_ref, *, L, tq, conv_ch):
    t = pl.program_id(1)

    @pl.when(t == 0)
    def _():
        apad_ref[0:CONV_HALO, :] = jnp.zeros((CONV_HALO, conv_ch), F32)
        apad_ref[CONV_HALO:CONV_HALO + L, :] = a_ref[0]
        apad_ref[CONV_HALO + L:CONV_HALO + L + CONV_HALO, :] = jnp.zeros((CONV_HALO, conv_ch), F32)

    t0 = pl.multiple_of(t * tq, tq)
    win = tq + 8 * CONV_TAP_GROUPS
    for c in range(conv_ch // LANES):
        cols = slice(c * LANES, (c + 1) * LANES)
        full = apad_ref[pl.ds(t0, win), cols]
        acc = jnp.zeros((tq, LANES), F32) + cb_ref[:, cols]
        for r in range(8):
            shifted = full if r == 0 else pltpu.roll(full, win - r, axis=0)
            for j in range(CONV_TAP_GROUPS):
                m = 8 * j + r
                if m == 0:
                    continue
                acc = acc + shifted[8 * j:8 * j + tq, :] * cw_ref[m:m + 1, cols]
        act_ref[:, cols] = acc
    four = _dot(d_ref[...], y_ref[0])
    conv = act_ref[...]
    mu = jnp.mean(conv, axis=-1, keepdims=True)
    cen = conv - mu
    var = jnp.mean(cen * cen, axis=-1, keepdims=True)
    ln = cen * lax.rsqrt(var + NORM_EPS) * lg_ref[...] + lb_ref[...]
    act = (ln * _sigmoid(ln)).astype(BF16)
    mix = _dot(act, wo_ref[0:conv_ch, :]) + _dot(four.astype(BF16), wo_ref[conv_ch:, :])
    o_ref[0] = x_ref[0] + g1_ref[0] * mix


def _ab_out(a, ystack, dmat, x, g1, per_sample, conv_w, conv_b, ln_g, ln_b, w_out, tq):
    n, L, d = x.shape
    conv_ch = a.shape[-1]
    four_ch = ystack.shape[-1]
    kern = functools.partial(_ab_out_kernel, L=L, tq=tq, conv_ch=conv_ch)
    vec = lambda b, t: (0, 0)
    return pl.pallas_call(
        kern,
        grid=(n, L // tq),
        in_specs=[
            pl.BlockSpec((1, L, conv_ch), lambda b, t: (b, 0, 0)),
            pl.BlockSpec((1, 2 * L, four_ch), lambda b, t: (b, 0, 0)),
            pl.BlockSpec((tq, 2 * L), lambda b, t: (t, 0)),
            pl.BlockSpec((1, tq, d), lambda b, t: (b, t, 0)),
            _mod_spec(per_sample, d),
            pl.BlockSpec(conv_w.shape, vec),
            pl.BlockSpec((1, conv_ch), vec),
            pl.BlockSpec((1, conv_ch), vec),
            pl.BlockSpec((1, conv_ch), vec),
            pl.BlockSpec(w_out.shape, vec),
        ],
        out_specs=pl.BlockSpec((1, tq, d), lambda b, t: (b, t, 0)),
        out_shape=jax.ShapeDtypeStruct((n, L, d), F32),
        scratch_shapes=[
            pltpu.VMEM((L + 2 * CONV_HALO, conv_ch), F32),
            pltpu.VMEM((tq, conv_ch), F32),
        ],
        compiler_params=_cparams("arbitrary", "arbitrary"),
        name="ab_out",
    )(a, ystack, dmat, x, g1, conv_w, conv_b, ln_g, ln_b, w_out)


def _dft_tables(L, group_ch):
    k = np.arange(L, dtype=np.int64)
    ang = 2.0 * np.pi * ((k[:, None] * k[None, :]) % L).astype(np.float64) / L
    dmat = np.concatenate([np.cos(ang), -np.sin(ang)], axis=1) / np.sqrt(L)
    m = np.arange(group_ch, dtype=np.int64)
    angc = 2.0 * np.pi * ((m[:, None] * m[None, :]) % group_ch).astype(np.float64) / group_ch
    cs = np.concatenate([np.cos(angc), np.sin(angc)], axis=1) / np.sqrt(group_ch)
    return jnp.asarray(dmat.astype(np.float32)).astype(BF16), jnp.asarray(cs.astype(np.float32)).astype(BF16)


def _conv_fourier_layer(x, norm_g, sh, sc, g1, per_sample, w_in, conv_w, conv_b, ln_g, ln_b, w_out, tl, tq):
    n, L, d = x.shape
    conv_ch = conv_w.shape[-1]
    four_ch = w_in.shape[1] - 2 * conv_ch
    dmat, cs = _dft_tables(L, four_ch // FOURIER_GROUPS)
    a, y = _ab_in(x, norm_g.reshape(1, d), sh, sc, per_sample, w_in.astype(BF16), cs, conv_ch, tl)
    cw = jnp.concatenate([jnp.zeros((1, conv_ch), F32), conv_w], axis=0)
    return _ab_out(a, y.reshape(n, 2 * L, four_ch), dmat, x, g1, per_sample, cw,
                   conv_b.reshape(1, conv_ch), ln_g.reshape(1, conv_ch), ln_b.reshape(1, conv_ch),
                   w_out.astype(BF16), tq)


def _rope_tables(L):
    rows = L // GRID_W
    row = np.repeat(np.arange(rows, dtype=np.float32), GRID_W)
    col = np.tile(np.arange(GRID_W, dtype=np.float32), rows)
    nf = QK_ROPE // 4
    inv_freq = (np.float32(ROPE_BASE) ** (-np.arange(nf, dtype=np.float32) / np.float32(nf))).astype(np.float32)
    ang = np.concatenate([row[:, None] * inv_freq, col[:, None] * inv_freq], axis=-1).astype(np.float32)
    cos = np.repeat(np.cos(ang.astype(np.float64)), 2, axis=-1)
    sin = np.repeat(np.sin(ang.astype(np.float64)), 2, axis=-1)
    cos_t = np.zeros((L, HEAD_PAD), np.float32)
    sin_t = np.zeros((L, HEAD_PAD), np.float32)
    cos_t[:, :QK_NOPE] = 1.0
    cos_t[:, QK_NOPE:QK_HEAD] = cos
    sin_t[:, QK_NOPE:QK_HEAD] = sin
    return jnp.asarray(cos_t), jnp.asarray(sin_t)


def _pair_swap(w):
    w2 = w.reshape(w.shape[:-1] + (QK_ROPE // 2, 2))
    return jnp.stack([-w2[..., 1], w2[..., 0]], axis=-1).reshape(w.shape)


def _mla_weights(w_in, w_uq, w_ukv):
    d = w_in.shape[0]
    h = MLA_HEADS
    zeros = lambda *s: jnp.zeros(s, F32)
    w_kr = w_in[:, Q_LORA + KV_LORA:]
    kr_a = jnp.concatenate([zeros(d, QK_NOPE), w_kr, zeros(d, HEAD_PAD - QK_HEAD)], axis=1)
    kr_b = jnp.concatenate([zeros(d, QK_NOPE), _pair_swap(w_kr), zeros(d, HEAD_PAD - QK_HEAD)], axis=1)
    w_in_ext = jnp.concatenate([w_in[:, :Q_LORA + KV_LORA], kr_a, kr_b], axis=1)
    wq = w_uq.reshape(Q_LORA, h, QK_HEAD)
    pad = zeros(Q_LORA, h, HEAD_PAD - QK_HEAD)
    wq_a = jnp.concatenate([wq, pad], axis=-1).reshape(Q_LORA, h * HEAD_PAD)
    wq_b = jnp.concatenate([zeros(Q_LORA, h, QK_NOPE), _pair_swap(wq[..., QK_NOPE:]), pad],
                           axis=-1).reshape(Q_LORA, h * HEAD_PAD)
    wkv = w_ukv.reshape(KV_LORA, h, QK_NOPE + V_HEAD)
    wk = jnp.concatenate([wkv[..., :QK_NOPE], zeros(KV_LORA, h, HEAD_PAD - QK_NOPE)],
                         axis=-1).reshape(KV_LORA, h * HEAD_PAD)
    wv = wkv[..., QK_NOPE:].reshape(KV_LORA, h * V_HEAD)
    return (w_in_ext.astype(BF16), jnp.concatenate([wq_a, wq_b], axis=1).astype(BF16),
            jnp.concatenate([wk, wv], axis=1).astype(BF16))


def _rms(x, g):
    return x * lax.rsqrt(jnp.mean(x * x, axis=-1, keepdims=True) + NORM_EPS) * g


def _mla_proj_kernel(*refs, latent):
    if latent:
        (x_ref, g_ref, sh_ref, sc_ref, win_ref, qg_ref, kvg_ref, wq_ref, wkv_ref, cos_ref, sin_ref,
         q_ref, k_ref, v_ref) = refs
    else:
        (x_ref, g_ref, sh_ref, sc_ref, win_ref, kvg_ref, wkv_ref, k_ref, v_ref) = refs
    hp = MLA_HEADS * HEAD_PAD
    u = _norm_modulate(x_ref[0], g_ref[...], sh_ref[0], sc_ref[0]).astype(BF16)
    z = _dot(u, win_ref[...])
    ckv = _rms(z[:, Q_LORA:Q_LORA + KV_LORA], kvg_ref[...]).astype(BF16)
    kr = z[:, Q_LORA + KV_LORA:Q_LORA + KV_LORA + HEAD_PAD]
    if latent:
        cos = cos_ref[...]
        sin = sin_ref[...]
        kr = kr * cos + z[:, Q_LORA + KV_LORA + HEAD_PAD:] * sin
        cq = _rms(z[:, :Q_LORA], qg_ref[...]).astype(BF16)
    pair_w = 2 * HEAD_PAD
    for hpair in range(MLA_HEADS // 2):
        pcols = slice(hpair * pair_w, (hpair + 1) * pair_w)
        if latent:
            qa = _dot(cq, wq_ref[:, pcols])
            qb = _dot(cq, wq_ref[:, hp + hpair * pair_w:hp + (hpair + 1) * pair_w])
        kn = _dot(ckv, wkv_ref[:, pcols])
        for par in range(2):
            h = 2 * hpair + par
            cols = slice(par * HEAD_PAD, (par + 1) * HEAD_PAD)
            if latent:
                q_ref[0, h] = ((qa[:, cols] * cos + qb[:, cols] * sin) * SCORE_SCALE).T.astype(BF16)
            k_ref[0, h] = (kn[:, cols] + kr).astype(BF16)
    vv = _dot(ckv, wkv_ref[:, hp:])
    for hpair in range(MLA_HEADS // 2):
        v_ref[0, hpair] = vv[:, hpair * 2 * V_HEAD:(hpair + 1) * 2 * V_HEAD].T.astype(BF16)


def _mla_proj(x, g, sh, sc, per_sample, w_in_ext, q_g, kv_g, wq, wkv, cos_t, sin_t, latent, tl):
    n, L, d = x.shape
    vec = lambda b, t: (0, 0)
    row = lambda b, t: (b, t, 0)
    in_specs = [
        pl.BlockSpec((1, tl, d), row),
        pl.BlockSpec((1, d), vec),
        _mod_spec(per_sample, d),
        _mod_spec(per_sample, d),
        pl.BlockSpec(w_in_ext.shape, vec),
    ]
    args = [x, g, sh, sc, w_in_ext]
    if latent:
        in_specs += [pl.BlockSpec((1, Q_LORA), vec)]
        args += [q_g]
    in_specs += [pl.BlockSpec((1, KV_LORA), vec)]
    args += [kv_g]
    if latent:
        in_specs += [pl.BlockSpec(wq.shape, vec)]
        args += [wq]
    in_specs += [pl.BlockSpec(wkv.shape, vec)]
    args += [wkv]
    nh, nhp = MLA_HEADS, MLA_HEADS // 2
    head_row = lambda b, t: (b, 0, t, 0)
    head_col = lambda b, t: (b, 0, 0, t)
    out_specs = [pl.BlockSpec((1, nh, tl, HEAD_PAD), head_row), pl.BlockSpec((1, nhp, V_ROWS, tl), head_col)]
    out_shape = [jax.ShapeDtypeStruct((n, nh, L, HEAD_PAD), BF16),
                 jax.ShapeDtypeStruct((n, nhp, V_ROWS, L), BF16)]
    if latent:
        in_specs += [pl.BlockSpec((tl, HEAD_PAD), lambda b, t: (t, 0))] * 2
        args += [cos_t, sin_t]
        out_specs = [pl.BlockSpec((1, nh, HEAD_PAD, tl), head_col)] + out_specs
        out_shape = [jax.ShapeDtypeStruct((n, nh, HEAD_PAD, L), BF16)] + out_shape
    return pl.pallas_call(
        functools.partial(_mla_proj_kernel, latent=latent),
        grid=(n, L // tl),
        in_specs=in_specs,
        out_specs=out_specs,
        out_shape=out_shape,
        compiler_params=_cparams("arbitrary", "arbitrary"),
        name="mla_proj_lat" if latent else "mla_proj_ctx",
    )(*args)


def _mla_attn_kernel(q_ref, kc_ref, kl_ref, vc_ref, vl_ref, x_ref, g1_ref, wo_ref, o_ref, oh_ref, s_ref,
                     *, n_ctx, n_lat):
    tq = q_ref.shape[3]
    chunks = [(True, 0, n_ctx, 0)]
    chunks += [(False, k0, KEY_CHUNK, n_ctx + k0) for k0 in range(0, n_lat, KEY_CHUNK)]
    even_rows = lax.broadcasted_iota(jnp.int32, (2 * V_HEAD, tq), 0) < V_HEAD

    def scores(h, par):
        qt = q_ref[0, h]
        mx = jnp.full((1, tq), -jnp.inf, F32)
        for is_ctx, k0, kn, r0 in chunks:
            k_ref = kc_ref if is_ctx else kl_ref
            s = _dot(k_ref[0, h, k0:k0 + kn, :], qt)
            s_ref[par, r0:r0 + kn, :] = s
            mx = jnp.maximum(mx, jnp.max(s, axis=0, keepdims=True))
        return mx

    def values(hpair, par, mx):
        den = jnp.zeros((1, tq), F32)
        acc = jnp.zeros((V_ROWS, tq), F32)
        for is_ctx, k0, kn, r0 in chunks:
            v_ref = vc_ref if is_ctx else vl_ref
            p = jnp.exp2(s_ref[par, r0:r0 + kn, :] - mx)
            den = den + jnp.sum(p, axis=0, keepdims=True)
            acc = acc + _dot(v_ref[0, hpair, :, k0:k0 + kn], p.astype(BF16))
        return acc * (1.0 / den)

    def head_pair(hpair, mx_even, last):
        mx_odd = scores(2 * hpair + 1, 1)
        out_even = values(hpair, 0, mx_even)
        mx_next = mx_even if last else scores(2 * hpair + 2, 0)
        out_odd = values(hpair, 1, mx_odd)
        oh_ref[hpair] = jnp.where(even_rows, out_even, out_odd).T.astype(BF16)
        return mx_next

    n_pairs = MLA_HEADS // 2
    mx0 = lax.fori_loop(0, n_pairs - 1, lambda i, mx: head_pair(i, mx, False), scores(0, 0))
    head_pair(n_pairs - 1, mx0, True)
    heads = jnp.concatenate([oh_ref[i] for i in range(n_pairs)], axis=1)
    o_ref[0] = x_ref[0] + g1_ref[0] * _dot(heads, wo_ref[...])


def _mla_attn(q, k_ctx, k_lat, v_ctx, v_lat, x, g1, w_o, tq):
    n, L, d = x.shape
    n_ctx = k_ctx.shape[2]
    nh, nhp = MLA_HEADS, MLA_HEADS // 2
    samp = lambda b, t: (b, 0, 0, 0)
    row = lambda b, t: (b, t, 0)
    assert L % KEY_CHUNK == 0
    return pl.pallas_call(
        functools.partial(_mla_attn_kernel, n_ctx=n_ctx, n_lat=L),
        grid=(n, L // tq),
        in_specs=[
            pl.BlockSpec((1, nh, HEAD_PAD, tq), lambda b, t: (b, 0, 0, t)),
            pl.BlockSpec((1, nh, n_ctx, HEAD_PAD), samp),
            pl.BlockSpec((1, nh, L, HEAD_PAD), samp),
            pl.BlockSpec((1, nhp, V_ROWS, n_ctx), samp),
            pl.BlockSpec((1, nhp, V_ROWS, L), samp, pipeline_mode=pl.Buffered(1)),
            pl.BlockSpec((1, tq, d), row),
            _mod_spec(True, d),
            pl.BlockSpec(w_o.shape, lambda b, t: (0, 0), pipeline_mode=pl.Buffered(1)),
        ],
        out_specs=pl.BlockSpec((1, tq, d), row),
        out_shape=jax.ShapeDtypeStruct((n, L, d), F32),
        scratch_shapes=[pltpu.VMEM((nhp, tq, 2 * V_HEAD), BF16),
                        pltpu.VMEM((2, n_ctx + L, tq), F32)],
        compiler_params=_cparams("arbitrary", "arbitrary"),
        name="mla_attn",
    )(q, k_ctx, k_lat, v_ctx, v_lat, x, g1, w_o)


def _route_kernel(x_ref, g_ref, sh_ref, sc_ref, wr_ref, u_ref, slot_ref, aff_ref, slott_ref, first_ref,
                  afft_ref, tri_ref, *, L, tl, cap, gs):
    s = pl.program_id(1)
    t = pl.program_id(2)
    ne = N_EXPERTS
    rows = gs * ne

    @pl.when((pl.program_id(0) == 0) & (s == 0) & (t == 0))
    def _():
        r = lax.broadcasted_iota(jnp.int32, (PREFIX_BLOCK, PREFIX_BLOCK), 0)
        c = lax.broadcasted_iota(jnp.int32, (PREFIX_BLOCK, PREFIX_BLOCK), 1)
        tri_ref[...] = jnp.where(r < c, 1.0, 0.0).astype(BF16)

    u = _norm_modulate(x_ref[0], g_ref[...], sh_ref[0], sc_ref[0]).astype(BF16)
    u_ref[0] = u
    logits = _dot_nt(wr_ref[...], u)
    e = jnp.exp(logits - jnp.max(logits, axis=0, keepdims=True))
    t0 = pl.multiple_of(t * tl, tl)
    afft_ref[pl.ds(pl.multiple_of(s * ne, ne), ne), pl.ds(t0, tl)] = e / jnp.sum(e, axis=0, keepdims=True)

    @pl.when((s == gs - 1) & (t == pl.num_programs(2) - 1))
    def _():
        aff = afft_ref[...]

        def as_f32(bits):
            return lax.bitcast_convert_type(bits, F32)

        def count_ge(th):
            return jnp.sum(jnp.where(aff >= th, 1.0, 0.0), axis=1, keepdims=True)

        def body(_, carry):
            lo, hi = carry
            mid = lo + ((hi - lo) >> 1)
            ok = count_ge(as_f32(mid)) >= cap
            return jnp.where(ok, mid, lo), jnp.where(ok, hi, mid)

        lo0 = jnp.zeros((rows, 1), jnp.int32)
        hi0 = jnp.full((rows, 1), 0x7F800000, jnp.int32)
        lo_bits, hi_bits = lax.fori_loop(0, 31, body, (lo0, hi0))
        gt = aff >= as_f32(hi_bits)
        eq = (aff >= as_f32(lo_bits)) & jnp.logical_not(gt)
        need = cap - jnp.sum(jnp.where(gt, 1.0, 0.0), axis=1, keepdims=True)

        def prefix_count(mask):
            ones = jnp.where(mask, 1.0, 0.0)
            base = jnp.zeros((rows, 1), F32)
            parts = []
            for j in range(L // PREFIX_BLOCK):
                blk = ones[:, j * PREFIX_BLOCK:(j + 1) * PREFIX_BLOCK]
                parts.append(_dot(blk.astype(BF16), tri_ref[...]) + base)
                base = base + jnp.sum(blk, axis=1, keepdims=True)
            return parts[0] if len(parts) == 1 else jnp.concatenate(parts, axis=1)

        sel = gt | (eq & (prefix_count(eq) < need))
        pos = prefix_count(sel)
        tok = lax.broadcasted_iota(jnp.int32, (L, LANES), 0)
        blk_start = lax.broadcasted_iota(jnp.int32, (L, LANES), 1) * FIRST_BLOCK
        before = jnp.where(tok < blk_start, 1.0, 0.0).astype(BF16)
        first_i = _dot(jnp.where(sel, 1.0, 0.0).astype(BF16), before).astype(jnp.int32)
        slot = jnp.where(sel, pos, -1.0)
        slot_i = slot.astype(jnp.int32)
        fill = jnp.full((LANES - ne, L), -1.0, F32)
        for si in range(gs):
            first_ref[si] = first_i[si * ne:(si + 1) * ne]
            for ex in range(ne):
                r = si * ne + ex
                slot_ref[si, ex] = slot_i[r:r + 1, :]
                aff_ref[si, ex] = aff[r:r + 1, :]
            slott_ref[si] = jnp.concatenate([slot[si * ne:(si + 1) * ne], fill], axis=0).T


def _route(x, g, sh, sc, per_sample, w_router_t, cap, tl, gs):
    n, L, d = x.shape
    ne = N_EXPERTS
    assert n % gs == 0 and L % tl == 0
    vec = lambda gi, s, t: (0, 0)
    row = lambda gi, s, t: (gi * gs + s, t, 0)
    group4 = lambda gi, s, t: (gi, 0, 0, 0)
    if per_sample:
        mod_spec = pl.BlockSpec((1, 1, d), lambda gi, s, t: (gi * gs + s, 0, 0))
    else:
        mod_spec = pl.BlockSpec((1, 1, d), lambda gi, s, t: (0, 0, 0))
    return pl.pallas_call(
        functools.partial(_route_kernel, L=L, tl=tl, cap=cap, gs=gs),
        grid=(n // gs, gs, L // tl),
        in_specs=[
            pl.BlockSpec((1, tl, d), row),
            pl.BlockSpec((1, d), vec),
            mod_spec,
            mod_spec,
            pl.BlockSpec((ne, d), vec),
        ],
        out_specs=[
            pl.BlockSpec((1, tl, d), row),
            pl.BlockSpec((gs, ne, 1, L), group4),
            pl.BlockSpec((gs, ne, 1, L), group4),
            pl.BlockSpec((gs, L, LANES), lambda gi, s, t: (gi, 0, 0)),
            pl.BlockSpec((gs, ne, LANES), lambda gi, s, t: (gi, 0, 0)),
        ],
        out_shape=[
            jax.ShapeDtypeStruct((n, L, d), BF16),
            jax.ShapeDtypeStruct((n, ne, 1, L), jnp.int32),
            jax.ShapeDtypeStruct((n, ne, 1, L), F32),
            jax.ShapeDtypeStruct((n, L, LANES), F32),
            jax.ShapeDtypeStruct((n, ne, LANES), jnp.int32),
        ],
        scratch_shapes=[pltpu.VMEM((gs * ne, L), F32), pltpu.VMEM((PREFIX_BLOCK, PREFIX_BLOCK), BF16)],
        compiler_params=_cparams("arbitrary", "arbitrary", "arbitrary"),
        name="route",
    )(x, g, sh, sc, w_router_t)


def _expert_kernel(first_ref, u_ref, slot_ref, aff_ref, w1_ref, w3_ref, w2_ref, y_ref, w1b, w3b, w2b,
                   xg_ref, gate_ref, *, nb, cap):
    e = pl.program_id(0)
    bb = pl.program_id(1)

    @pl.when(bb == 0)
    def _():
        w1b[...] = w1_ref[0, 0].astype(BF16)
        w3b[...] = w3_ref[0, 0].astype(BF16)
        w2b[...] = w2_ref[0, 0].astype(BF16)

    L = u_ref.shape[1]
    nblk = L // PREFIX_BLOCK

    def gather_dense():
        slot_iota = lax.broadcasted_iota(jnp.int32, (cap, L), 0)
        for s in range(nb):
            hit = slot_ref[s, 0] == slot_iota
            xg_ref[s * cap:(s + 1) * cap, :] = _dot(jnp.where(hit, 1.0, 0.0).astype(BF16), u_ref[s])
            gate_ref[s * cap:(s + 1) * cap, :] = jnp.broadcast_to(
                jnp.sum(jnp.where(hit, aff_ref[s, 0], 0.0), axis=1, keepdims=True), (cap, LANES))

    if cap <= GATHER_ROWS:
        gather_dense()
    else:
        starts, fits = [], None
        for s in range(nb):
            base = ((e * pl.num_programs(1) + bb) * nb + s) * FIRST_STRIDE
            per = PREFIX_BLOCK // FIRST_BLOCK
            for j in range(nblk):
                start = (first_ref[base + per * j] // SUBLANES) * SUBLANES
                ok = first_ref[base + per * (j + 1)] - start <= GATHER_ROWS
                starts.append(start)
                fits = ok if fits is None else jnp.logical_and(fits, ok)

        @pl.when(fits)
        def _():
            xg_ref[...] = jnp.zeros(xg_ref.shape, F32)
            gate_ref[...] = jnp.zeros(gate_ref.shape, F32)
            row_iota = lax.broadcasted_iota(jnp.int32, (GATHER_ROWS, PREFIX_BLOCK), 0)
            for s in range(nb):
                for j in range(nblk):
                    toks = slice(j * PREFIX_BLOCK, (j + 1) * PREFIX_BLOCK)
                    start = pl.multiple_of(starts[s * nblk + j], SUBLANES)
                    hit = slot_ref[s, 0, :, toks] == (row_iota + start)
                    rows = pl.ds(s * cap + start, GATHER_ROWS)
                    xg_ref[rows, :] += _dot(jnp.where(hit, 1.0, 0.0).astype(BF16), u_ref[s, toks, :])
                    gate_ref[rows, :] += jnp.broadcast_to(
                        jnp.sum(jnp.where(hit, aff_ref[s, 0, :, toks], 0.0), axis=1, keepdims=True),
                        (GATHER_ROWS, LANES))

        @pl.when(jnp.logical_not(fits))
        def _():
            gather_dense()

    xg = xg_ref[0:nb * cap, :].astype(BF16)
    gate = gate_ref[0:nb * cap, 0:1]
    h1 = _dot(xg, w1b[...])
    hid = (h1 * _sigmoid(h1) * _dot(xg, w3b[...])).astype(BF16)
    y = (_dot(hid, w2b[...]) * gate).astype(BF16)
    for s in range(nb):
        y_ref[0, s] = y[s * cap:(s + 1) * cap]


def _experts(u, slot, aff, first, layer, w1, w3, w2, cap, nb):
    n, L, d = u.shape
    _, ne, _, ff = w1.shape
    assert L // FIRST_BLOCK < FIRST_STRIDE
    first_flat = jnp.transpose(first[:, :, :FIRST_STRIDE], (1, 0, 2)).reshape(-1)
    wspec = lambda shape: pl.BlockSpec((1, 1) + shape, lambda e, b, tab: (layer, e, 0, 0))
    rows = nb * cap + GATHER_ROWS
    grid_spec = pltpu.PrefetchScalarGridSpec(
        num_scalar_prefetch=1,
        grid=(ne, n // nb),
        in_specs=[
            pl.BlockSpec((nb, L, d), lambda e, b, tab: (b, 0, 0)),
            pl.BlockSpec((nb, 1, 1, L), lambda e, b, tab: (b, e, 0, 0)),
            pl.BlockSpec((nb, 1, 1, L), lambda e, b, tab: (b, e, 0, 0)),
            wspec((d, ff)), wspec((d, ff)), wspec((ff, d)),
        ],
        out_specs=pl.BlockSpec((1, nb, cap, d), lambda e, b, tab: (e, b, 0, 0)),
        scratch_shapes=[pltpu.VMEM((d, ff), BF16), pltpu.VMEM((d, ff), BF16), pltpu.VMEM((ff, d), BF16),
                        pltpu.VMEM((rows, d), F32), pltpu.VMEM((rows, LANES), F32)],
    )
    return pl.pallas_call(
        functools.partial(_expert_kernel, nb=nb, cap=cap),
        grid_spec=grid_spec,
        out_shape=jax.ShapeDtypeStruct((ne, n, cap, d), BF16),
        compiler_params=_cparams("arbitrary", "arbitrary"),
        name="experts",
    )(first_flat, u, slot, aff, w1, w3, w2)


def _combine_kernel(*refs, cap, final):
    if final:
        first_ref, y_ref, st_ref, x_ref, g2_ref, fg_ref, o_ref = refs
    else:
        first_ref, y_ref, st_ref, x_ref, g2_ref, o_ref = refs
    b = pl.program_id(0)
    t = pl.program_id(1)
    tl = x_ref.shape[1]
    d = x_ref.shape[2]
    ne = N_EXPERTS

    def finish(rows, moe):
        out = x_ref[0, rows, :] + g2_ref[0] * moe
        if final:
            out = _rms(out, fg_ref[...])
        o_ref[0, rows, :] = out

    def scatter_dense():
        st = st_ref[0]
        lane = lax.broadcasted_iota(jnp.int32, (tl, cap), 1).astype(F32)
        onehot = jnp.concatenate(
            [jnp.where(st[:, e:e + 1] == lane, 1.0, 0.0).astype(BF16) for e in range(ne)], axis=1)
        finish(slice(0, tl), _dot(onehot, y_ref[...].reshape(ne * cap, d)))

    if cap <= SCATTER_ROWS:
        scatter_dense()
        return

    nsub = tl // FIRST_BLOCK
    starts, fits = [], None
    for q in range(nsub):
        for e in range(ne):
            entry = (b * ne + e) * FIRST_STRIDE + t * nsub + q
            start = jnp.minimum((first_ref[entry] // BF16_ROWS) * BF16_ROWS, cap - SCATTER_ROWS)
            ok = first_ref[entry + 1] - start <= SCATTER_ROWS
            starts.append(start)
            fits = ok if fits is None else jnp.logical_and(fits, ok)

    @pl.when(fits)
    def _():
        per = LANES // SCATTER_ROWS
        lane = lax.broadcasted_iota(jnp.int32, (FIRST_BLOCK, LANES), 1).astype(F32)
        for q in range(nsub):
            rows = slice(q * FIRST_BLOCK, (q + 1) * FIRST_BLOCK)
            st = st_ref[0, rows, :]
            hots, wins = [], []
            for grp in range(ne // per):
                group = [grp * per + k for k in range(per)]
                first = [pl.multiple_of(starts[q * ne + e], BF16_ROWS) for e in group]
                for e, start in zip(group, first):
                    wins.append(y_ref[e, 0, pl.ds(start, SCATTER_ROWS), :])
                want = None
                for k in reversed(range(per)):
                    e = group[k]
                    mine = st[:, e:e + 1] - (first[k] - k * SCATTER_ROWS).astype(F32)
                    want = mine if want is None else jnp.where(lane < (k + 1) * SCATTER_ROWS, mine, want)
                hots.append(jnp.where(want == lane, 1.0, 0.0).astype(BF16))
            finish(rows, _dot(jnp.concatenate(hots, axis=1), jnp.concatenate(wins, axis=0)))

    @pl.when(jnp.logical_not(fits))
    def _():
        scatter_dense()


def _combine(y, slot_t, first, x, g2, per_sample, cap, tl, final_g=None):
    n, L, d = x.shape
    ne = N_EXPERTS
    final = final_g is not None
    assert L // FIRST_BLOCK < FIRST_STRIDE and tl % FIRST_BLOCK == 0
    first_flat = first[:, :, :FIRST_STRIDE].reshape(-1)
    if per_sample:
        mod_spec = pl.BlockSpec((1, 1, d), lambda b, t, tab: (b, 0, 0))
    else:
        mod_spec = pl.BlockSpec((1, 1, d), lambda b, t, tab: (0, 0, 0))
    in_specs = [
        pl.BlockSpec((ne, 1, cap, d), lambda b, t, tab: (0, b, 0, 0)),
        pl.BlockSpec((1, tl, LANES), lambda b, t, tab: (b, t, 0)),
        pl.BlockSpec((1, tl, d), lambda b, t, tab: (b, t, 0)),
        mod_spec,
    ]
    args = [first_flat, y, slot_t, x, g2]
    if final:
        in_specs += [pl.BlockSpec((1, d), lambda b, t, tab: (0, 0))]
        args += [final_g.reshape(1, d)]
    grid_spec = pltpu.PrefetchScalarGridSpec(
        num_scalar_prefetch=1,
        grid=(n, L // tl),
        in_specs=in_specs,
        out_specs=pl.BlockSpec((1, tl, d), lambda b, t, tab: (b, t, 0)),
    )
    return pl.pallas_call(
        functools.partial(_combine_kernel, cap=cap, final=final),
        grid_spec=grid_spec,
        out_shape=jax.ShapeDtypeStruct((n, L, d), F32),
        compiler_params=_cparams("arbitrary", "arbitrary"),
        name="combine",
    )(*args)


def _ec_moe_layer(x, norm_g, sh, sc, g2, per_sample, layer, w_router, w1, w3, w2, tl, nb, gs, final_g=None):
    n, L, d = x.shape
    cap = EC_CAPACITY_FACTOR * L // N_EXPERTS
    u, slot, aff, slot_t, first = _route(x, norm_g.reshape(1, d), sh, sc, per_sample,
                                         w_router.T.astype(BF16), cap, tl, gs)
    y = _experts(u, slot, aff, first, layer, w1, w3, w2, cap, nb)
    return _combine(y, slot_t, first, x, g2, per_sample, cap, tl, final_g)


def kernel(x, c, ctx, c_ctx, mod_w, mod_b, norm1_g, norm2_g, ab_w_in, ab_conv_w, ab_conv_b, ab_ln_g, ab_ln_b, ab_w_out, mla_w_in, mla_q_norm_g, mla_kv_norm_g, mla_w_uq, mla_w_ukv, mla_w_o, moe_w_router, moe_w1, moe_w3, moe_w2, final_g):
    n, L, d = x.shape
    n_ctx = ctx.shape[1]
    depth = mod_w.shape[0]
    assert n < MOD_ROWS
    cvecs = jnp.concatenate([c, c_ctx[None, :], jnp.zeros((MOD_ROWS - n - 1, d), F32)], axis=0)
    mod = _modulation(cvecs, mod_w, mod_b)

    def mods(i, ctx_stream):
        rows = mod[i, n:n + 1] if ctx_stream else mod[i, :n]
        return [rows[:, k * d:(k + 1) * d].reshape(-1, 1, d) for k in range(N_MOD)]

    x_lat, x_ctx = x, ctx
    for i in range(depth):
        ctx_out = i < depth - 1
        last = i == depth - 1
        j = i // 2
        sh1, sc1, g1, sh2, sc2, g2 = mods(i, False)
        csh1, csc1, cg1, csh2, csc2, cg2 = mods(i, True)
        if i % 2 == 0:
            ab = (ab_w_in[j], ab_conv_w[j], ab_conv_b[j], ab_ln_g[j], ab_ln_b[j], ab_w_out[j])
            x_lat = _conv_fourier_layer(x_lat, norm1_g[i], sh1, sc1, g1, True, *ab, tl=512, tq=256)
            if ctx_out:
                x_ctx = _conv_fourier_layer(x_ctx, norm1_g[i], csh1, csc1, cg1, False, *ab,
                                            tl=n_ctx, tq=n_ctx)
        else:
            assert not ctx_out, "context-stream attention output is only needed for deeper stacks"
            w_in_ext, wq, wkv = _mla_weights(mla_w_in[j], mla_w_uq[j], mla_w_ukv[j])
            cos_t, sin_t = _rope_tables(L)
            g = norm1_g[i].reshape(1, d)
            q_g = mla_q_norm_g[j].reshape(1, Q_LORA)
            kv_g = mla_kv_norm_g[j].reshape(1, KV_LORA)
            q, k_lat, v_lat = _mla_proj(x_lat, g, sh1, sc1, True, w_in_ext, q_g, kv_g, wq, wkv,
                                        cos_t, sin_t, True, 512)
            k_ctx, v_ctx = _mla_proj(x_ctx, g, csh1, csc1, False,
                                     w_in_ext[:, :Q_LORA + KV_LORA + HEAD_PAD], None, kv_g, None, wkv,
                                     None, None, False, n_ctx)
            x_lat = _mla_attn(q, k_ctx, k_lat, v_ctx, v_lat, x_lat, g1, mla_w_o[j].astype(BF16), 512)
        moe = (i, moe_w_router[i], moe_w1, moe_w3, moe_w2)
        x_lat = _ec_moe_layer(x_lat, norm2_g[i], sh2, sc2, g2, True, *moe, tl=512, nb=2, gs=4,
                              final_g=final_g if last else None)
        if ctx_out:
            x_ctx = _ec_moe_layer(x_ctx, norm2_g[i], csh2, csc2, cg2, False, *moe, tl=n_ctx, nb=n, gs=n)
    return x_lat
```

```python
import functools

import numpy as np
import jax
import jax.numpy as jnp
from jax import lax
from jax.experimental import pallas as pl
from jax.experimental.pallas import tpu as pltpu

F32 = jnp.float32
BF16 = jnp.bfloat16

NORM_EPS = 1e-6
N_MOD = 6
GRID_W = 64
CONV_WIDTH = 31
CONV_HALO = 16
CONV_TAP_GROUPS = 4
FOURIER_GROUPS = 4
MLA_HEADS = 16
QK_NOPE = 64
QK_ROPE = 32
V_HEAD = 64
V_ROWS = 2 * V_HEAD
QK_HEAD = QK_NOPE + QK_ROPE
HEAD_PAD = 128
Q_LORA = 256
KV_LORA = 128
ROPE_BASE = 10000.0
SCORE_SCALE = float(QK_HEAD ** -0.5 * np.log2(np.e))
KEY_CHUNK = 512
N_EXPERTS = 16
PREFIX_BLOCK = 256
GATHER_ROWS = 128
SUBLANES = 8
FIRST_BLOCK = 128
FIRST_STRIDE = 32
SCATTER_ROWS = 64
BF16_ROWS = 16
EC_CAPACITY_FACTOR = 2
MOD_ROWS = 24
LANES = 128
VMEM_LIMIT = 56 * 1024 * 1024


def _cparams(*sem):
    return pltpu.CompilerParams(dimension_semantics=sem, vmem_limit_bytes=VMEM_LIMIT)


def _dot(a, b):
    return jnp.dot(a, b, preferred_element_type=F32)


def _dot_nt(a, b):
    return lax.dot_general(a, b, (((1,), (1,)), ((), ())), preferred_element_type=F32)


def _sigmoid(x):
    return 1.0 / (1.0 + jnp.exp(-x))


def _norm_modulate(x, g, shift, scale):
    ms = jnp.mean(x * x, axis=-1, keepdims=True)
    y = x * lax.rsqrt(ms + NORM_EPS) * g
    return y * (1.0 + scale) + shift


def _mod_kernel(c_ref, w_ref, b_ref, o_ref):
    c = c_ref[...]
    s = (c * _sigmoid(c)).astype(BF16)
    o_ref[0] = _dot(s, w_ref[0].astype(BF16)) + b_ref[0]


def _modulation(cvecs, mod_w, mod_b):
    depth, d, nd = mod_w.shape
    tn = 1024
    return pl.pallas_call(
        _mod_kernel,
        grid=(depth, nd // tn),
        in_specs=[
            pl.BlockSpec((MOD_ROWS, d), lambda i, j: (0, 0)),
            pl.BlockSpec((1, d, tn), lambda i, j: (i, 0, j)),
            pl.BlockSpec((1, 1, tn), lambda i, j: (i, 0, j)),
        ],
        out_specs=pl.BlockSpec((1, MOD_ROWS, tn), lambda i, j: (i, 0, j)),
        out_shape=jax.ShapeDtypeStruct((depth, MOD_ROWS, nd), F32),
        compiler_params=_cparams("arbitrary", "arbitrary"),
        name="modulation",
    )(cvecs, mod_w, mod_b.reshape(depth, 1, nd))


def _mod_spec(per_sample, d):
    if per_sample:
        return pl.BlockSpec((1, 1, d), lambda b, t: (b, 0, 0))
    return pl.BlockSpec((1, 1, d), lambda b, t: (0, 0, 0))


def _ab_in_kernel(x_ref, g_ref, sh_ref, sc_ref, w_ref, cs_ref, a_ref, y_ref, *, conv_ch, group_ch):
    u = _norm_modulate(x_ref[0], g_ref[...], sh_ref[0], sc_ref[0]).astype(BF16)
    z = _dot(u, w_ref[...])
    a_ref[0] = z[:, :conv_ch] * _sigmoid(z[:, conv_ch:2 * conv_ch])
    uf = z[:, 2 * conv_ch:].astype(BF16)
    for g in range(FOURIER_GROUPS):
        yg = _dot(uf[:, g * group_ch:(g + 1) * group_ch], cs_ref[...])
        y_ref[0, 0, :, g * group_ch:(g + 1) * group_ch] = yg[:, :group_ch].astype(BF16)
        y_ref[0, 1, :, g * group_ch:(g + 1) * group_ch] = yg[:, group_ch:].astype(BF16)


def _ab_in(x, g, sh, sc, per_sample, w_in, cs, conv_ch, tl):
    n, L, d = x.shape
    ab_in = w_in.shape[1]
    four_ch = ab_in - 2 * conv_ch
    group_ch = four_ch // FOURIER_GROUPS
    kern = functools.partial(_ab_in_kernel, conv_ch=conv_ch, group_ch=group_ch)
    return pl.pallas_call(
        kern,
        grid=(n, L // tl),
        in_specs=[
            pl.BlockSpec((1, tl, d), lambda b, t: (b, t, 0)),
            pl.BlockSpec((1, d), lambda b, t: (0, 0)),
            _mod_spec(per_sample, d),
            _mod_spec(per_sample, d),
            pl.BlockSpec((d, ab_in), lambda b, t: (0, 0)),
            pl.BlockSpec(cs.shape, lambda b, t: (0, 0)),
        ],
        out_specs=[
            pl.BlockSpec((1, tl, conv_ch), lambda b, t: (b, t, 0)),
            pl.BlockSpec((1, 2, tl, four_ch), lambda b, t: (b, 0, t, 0)),
        ],
        out_shape=[
            jax.ShapeDtypeStruct((n, L, conv_ch), F32),
            jax.ShapeDtypeStruct((n, 2, L, four_ch), BF16),
        ],
        compiler_params=_cparams("arbitrary", "arbitrary"),
        name="ab_in",
    )(x, g, sh, sc, w_in, cs)


def _ab_out_kernel(a_ref, y_ref, d_ref, x_ref, g1_ref, cw_ref, cb_ref, lg_ref, lb_ref, wo_ref,
                   o_ref, apad_ref, act_ref, *, L, tq, conv_ch):
    t = pl.program_id(1)

    @pl.when(t == 0)
    def _():
        apad_ref[0:CONV_HALO, :] = jnp.zeros((CONV_HALO, conv_ch), F32)
        apad_ref[CONV_HALO:CONV_HALO + L, :] = a_ref[0]
        apad_ref[CONV_HALO + L:CONV_HALO + L + CONV_HALO, :] = jnp.zeros((CONV_HALO, conv_ch), F32)

    t0 = pl.multiple_of(t * tq, tq)
    win = tq + 8 * CONV_TAP_GROUPS
    for c in range(conv_ch // LANES):
        cols = slice(c * LANES, (c + 1) * LANES)
        full = apad_ref[pl.ds(t0, win), cols]
        acc = jnp.zeros((tq, LANES), F32) + cb_ref[:, cols]
        for r in range(8):
            shifted = full if r == 0 else pltpu.roll(full, win - r, axis=0)
            for j in range(CONV_TAP_GROUPS):
                m = 8 * j + r
                if m == 0:
                    continue
                acc = acc + shifted[8 * j:8 * j + tq, :] * cw_ref[m:m + 1, cols]
        act_ref[:, cols] = acc
    four = _dot(d_ref[...], y_ref[0])
    conv = act_ref[...]
    mu = jnp.mean(conv, axis=-1, keepdims=True)
    cen = conv - mu
    var = jnp.mean(cen * cen, axis=-1, keepdims=True)
    ln = cen * lax.rsqrt(var + NORM_EPS) * lg_ref[...] + lb_ref[...]
    act = (ln * _sigmoid(ln)).astype(BF16)
    mix = _dot(act, wo_ref[0:conv_ch, :]) + _dot(four.astype(BF16), wo_ref[conv_ch:, :])
    o_ref[0] = x_ref[0] + g1_ref[0] * mix


def _ab_out(a, ystack, dmat, x, g1, per_sample, conv_w, conv_b, ln_g, ln_b, w_out, tq):
    n, L, d = x.shape
    conv_ch = a.shape[-1]
    four_ch = ystack.shape[-1]
    kern = functools.partial(_ab_out_kernel, L=L, tq=tq, conv_ch=conv_ch)
    vec = lambda b, t: (0, 0)
    return pl.pallas_call(
        kern,
        grid=(n, L // tq),
        in_specs=[
            pl.BlockSpec((1, L, conv_ch), lambda b, t: (b, 0, 0)),
            pl.BlockSpec((1, 2 * L, four_ch), lambda b, t: (b, 0, 0)),
            pl.BlockSpec((tq, 2 * L), lambda b, t: (t, 0)),
            pl.BlockSpec((1, tq, d), lambda b, t: (b, t, 0)),
            _mod_spec(per_sample, d),
            pl.BlockSpec(conv_w.shape, vec),
            pl.BlockSpec((1, conv_ch), vec),
            pl.BlockSpec((1, conv_ch), vec),
            pl.BlockSpec((1, conv_ch), vec),
            pl.BlockSpec(w_out.shape, vec),
        ],
        out_specs=pl.BlockSpec((1, tq, d), lambda b, t: (b, t, 0)),
        out_shape=jax.ShapeDtypeStruct((n, L, d), F32),
        scratch_shapes=[
            pltpu.VMEM((L + 2 * CONV_HALO, conv_ch), F32),
            pltpu.VMEM((tq, conv_ch), F32),
        ],
        compiler_params=_cparams("arbitrary", "arbitrary"),
        name="ab_out",
    )(a, ystack, dmat, x, g1, conv_w, conv_b, ln_g, ln_b, w_out)


def _dft_tables(L, group_ch):
    k = np.arange(L, dtype=np.int64)
    ang = 2.0 * np.pi * ((k[:, None] * k[None, :]) % L).astype(np.float64) / L
    dmat = np.concatenate([np.cos(ang), -np.sin(ang)], axis=1) / np.sqrt(L)
    m = np.arange(group_ch, dtype=np.int64)
    angc = 2.0 * np.pi * ((m[:, None] * m[None, :]) % group_ch).astype(np.float64) / group_ch
    cs = np.concatenate([np.cos(angc), np.sin(angc)], axis=1) / np.sqrt(group_ch)
    return jnp.asarray(dmat.astype(np.float32)).astype(BF16), jnp.asarray(cs.astype(np.float32)).astype(BF16)


def _conv_fourier_layer(x, norm_g, sh, sc, g1, per_sample, w_in, conv_w, conv_b, ln_g, ln_b, w_out, tl, tq):
    n, L, d = x.shape
    conv_ch = conv_w.shape[-1]
    four_ch = w_in.shape[1] - 2 * conv_ch
    dmat, cs = _dft_tables(L, four_ch // FOURIER_GROUPS)
    a, y = _ab_in(x, norm_g.reshape(1, d), sh, sc, per_sample, w_in.astype(BF16), cs, conv_ch, tl)
    cw = jnp.concatenate([jnp.zeros((1, conv_ch), F32), conv_w], axis=0)
    return _ab_out(a, y.reshape(n, 2 * L, four_ch), dmat, x, g1, per_sample, cw,
                   conv_b.reshape(1, conv_ch), ln_g.reshape(1, conv_ch), ln_b.reshape(1, conv_ch),
                   w_out.astype(BF16), tq)


def _rope_tables(L):
    rows = L // GRID_W
    row = np.repeat(np.arange(rows, dtype=np.float32), GRID_W)
    col = np.tile(np.arange(GRID_W, dtype=np.float32), rows)
    nf = QK_ROPE // 4
    inv_freq = (np.float32(ROPE_BASE) ** (-np.arange(nf, dtype=np.float32) / np.float32(nf))).astype(np.float32)
    ang = np.concatenate([row[:, None] * inv_freq, col[:, None] * inv_freq], axis=-1).astype(np.float32)
    cos = np.repeat(np.cos(ang.astype(np.float64)), 2, axis=-1)
    sin = np.repeat(np.sin(ang.astype(np.float64)), 2, axis=-1)
    cos_t = np.zeros((L, HEAD_PAD), np.float32)
    sin_t = np.zeros((L, HEAD_PAD), np.float32)
    cos_t[:, :QK_NOPE] = 1.0
    cos_t[:, QK_NOPE:QK_HEAD] = cos
    sin_t[:, QK_NOPE:QK_HEAD] = sin
    return jnp.asarray(cos_t), jnp.asarray(sin_t)


def _pair_swap(w):
    w2 = w.reshape(w.shape[:-1] + (QK_ROPE // 2, 2))
    return jnp.stack([-w2[..., 1], w2[..., 0]], axis=-1).reshape(w.shape)


def _mla_weights(w_in, w_uq, w_ukv):
    d = w_in.shape[0]
    h = MLA_HEADS
    zeros = lambda *s: jnp.zeros(s, F32)
    w_kr = w_in[:, Q_LORA + KV_LORA:]
    kr_a = jnp.concatenate([zeros(d, QK_NOPE), w_kr, zeros(d, HEAD_PAD - QK_HEAD)], axis=1)
    kr_b = jnp.concatenate([zeros(d, QK_NOPE), _pair_swap(w_kr), zeros(d, HEAD_PAD - QK_HEAD)], axis=1)
    w_in_ext = jnp.concatenate([w_in[:, :Q_LORA + KV_LORA], kr_a, kr_b], axis=1)
    wq = w_uq.reshape(Q_LORA, h, QK_HEAD)
    pad = zeros(Q_LORA, h, HEAD_PAD - QK_HEAD)
    wq_a = jnp.concatenate([wq, pad], axis=-1).reshape(Q_LORA, h * HEAD_PAD)
    wq_b = jnp.concatenate([zeros(Q_LORA, h, QK_NOPE), _pair_swap(wq[..., QK_NOPE:]), pad],
                           axis=-1).reshape(Q_LORA, h * HEAD_PAD)
    wkv = w_ukv.reshape(KV_LORA, h, QK_NOPE + V_HEAD)
    wk = jnp.concatenate([wkv[..., :QK_NOPE], zeros(KV_LORA, h, HEAD_PAD - QK_NOPE)],
                         axis=-1).reshape(KV_LORA, h * HEAD_PAD)
    wv = wkv[..., QK_NOPE:].reshape(KV_LORA, h * V_HEAD)
    return (w_in_ext.astype(BF16), jnp.concatenate([wq_a, wq_b], axis=1).astype(BF16),
            jnp.concatenate([wk, wv], axis=1).astype(BF16))


def _rms(x, g):
    return x * lax.rsqrt(jnp.mean(x * x, axis=-1, keepdims=True) + NORM_EPS) * g


def _mla_proj_kernel(*refs, latent):
    if latent:
        (x_ref, g_ref, sh_ref, sc_ref, win_ref, qg_ref, kvg_ref, wq_ref, wkv_ref, cos_ref, sin_ref,
         q_ref, k_ref, v_ref) = refs
    else:
        (x_ref, g_ref, sh_ref, sc_ref, win_ref, kvg_ref, wkv_ref, k_ref, v_ref) = refs
    hp = MLA_HEADS * HEAD_PAD
    u = _norm_modulate(x_ref[0], g_ref[...], sh_ref[0], sc_ref[0]).astype(BF16)
    z = _dot(u, win_ref[...])
    ckv = _rms(z[:, Q_LORA:Q_LORA + KV_LORA], kvg_ref[...]).astype(BF16)
    kr = z[:, Q_LORA + KV_LORA:Q_LORA + KV_LORA + HEAD_PAD]
    if latent:
        cos = cos_ref[...]
        sin = sin_ref[...]
        kr = kr * cos + z[:, Q_LORA + KV_LORA + HEAD_PAD:] * sin
        cq = _rms(z[:, :Q_LORA], qg_ref[...]).astype(BF16)
    pair_w = 2 * HEAD_PAD
    for hpair in range(MLA_HEADS // 2):
        pcols = slice(hpair * pair_w, (hpair + 1) * pair_w)
        if latent:
            qa = _dot(cq, wq_ref[:, pcols])
            qb = _dot(cq, wq_ref[:, hp + hpair * pair_w:hp + (hpair + 1) * pair_w])
        kn = _dot(ckv, wkv_ref[:, pcols])
        for par in range(2):
            h = 2 * hpair + par
            cols = slice(par * HEAD_PAD, (par + 1) * HEAD_PAD)
            if latent:
                q_ref[0, h] = ((qa[:, cols] * cos + qb[:, cols] * sin) * SCORE_SCALE).T.astype(BF16)
            k_ref[0, h] = (kn[:, cols] + kr).astype(BF16)
    vv = _dot(ckv, wkv_ref[:, hp:])
    for hpair in range(MLA_HEADS // 2):
        v_ref[0, hpair] = vv[:, hpair * 2 * V_HEAD:(hpair + 1) * 2 * V_HEAD].T.astype(BF16)


def _mla_proj(x, g, sh, sc, per_sample, w_in_ext, q_g, kv_g, wq, wkv, cos_t, sin_t, latent, tl):
    n, L, d = x.shape
    vec = lambda b, t: (0, 0)
    row = lambda b, t: (b, t, 0)
    in_specs = [
        pl.BlockSpec((1, tl, d), row),
        pl.BlockSpec((1, d), vec),
        _mod_spec(per_sample, d),
        _mod_spec(per_sample, d),
        pl.BlockSpec(w_in_ext.shape, vec),
    ]
    args = [x, g, sh, sc, w_in_ext]
    if latent:
        in_specs += [pl.BlockSpec((1, Q_LORA), vec)]
        args += [q_g]
    in_specs += [pl.BlockSpec((1, KV_LORA), vec)]
    args += [kv_g]
    if latent:
        in_specs += [pl.BlockSpec(wq.shape, vec)]
        args += [wq]
    in_specs += [pl.BlockSpec(wkv.shape, vec)]
    args += [wkv]
    nh, nhp = MLA_HEADS, MLA_HEADS // 2
    head_row = lambda b, t: (b, 0, t, 0)
    head_col = lambda b, t: (b, 0, 0, t)
    out_specs = [pl.BlockSpec((1, nh, tl, HEAD_PAD), head_row), pl.BlockSpec((1, nhp, V_ROWS, tl), head_col)]
    out_shape = [jax.ShapeDtypeStruct((n, nh, L, HEAD_PAD), BF16),
                 jax.ShapeDtypeStruct((n, nhp, V_ROWS, L), BF16)]
    if latent:
        in_specs += [pl.BlockSpec((tl, HEAD_PAD), lambda b, t: (t, 0))] * 2
        args += [cos_t, sin_t]
        out_specs = [pl.BlockSpec((1, nh, HEAD_PAD, tl), head_col)] + out_specs
        out_shape = [jax.ShapeDtypeStruct((n, nh, HEAD_PAD, L), BF16)] + out_shape
    return pl.pallas_call(
        functools.partial(_mla_proj_kernel, latent=latent),
        grid=(n, L // tl),
        in_specs=in_specs,
        out_specs=out_specs,
        out_shape=out_shape,
        compiler_params=_cparams("arbitrary", "arbitrary"),
        name="mla_proj_lat" if latent else "mla_proj_ctx",
    )(*args)


def _mla_attn_kernel(q_ref, kc_ref, kl_ref, vc_ref, vl_ref, x_ref, g1_ref, wo_ref, o_ref, oh_ref, s_ref,
                     *, n_ctx, n_lat):
    tq = q_ref.shape[3]
    chunks = [(True, 0, n_ctx, 0)]
    chunks += [(False, k0, KEY_CHUNK, n_ctx + k0) for k0 in range(0, n_lat, KEY_CHUNK)]
    even_rows = lax.broadcasted_iota(jnp.int32, (2 * V_HEAD, tq), 0) < V_HEAD

    def scores(h, par):
        qt = q_ref[0, h]
        mx = jnp.full((1, tq), -jnp.inf, F32)
        for is_ctx, k0, kn, r0 in chunks:
            k_ref = kc_ref if is_ctx else kl_ref
            s = _dot(k_ref[0, h, k0:k0 + kn, :], qt)
            s_ref[par, r0:r0 + kn, :] = s
            mx = jnp.maximum(mx, jnp.max(s, axis=0, keepdims=True))
        return mx

    def values(hpair, par, mx):
        den = jnp.zeros((1, tq), F32)
        acc = jnp.zeros((V_ROWS, tq), F32)
        for is_ctx, k0, kn, r0 in chunks:
            v_ref = vc_ref if is_ctx else vl_ref
            p = jnp.exp2(s_ref[par, r0:r0 + kn, :] - mx)
            den = den + jnp.sum(p, axis=0, keepdims=True)
            acc = acc + _dot(v_ref[0, hpair, :, k0:k0 + kn], p.astype(BF16))
        return acc * (1.0 / den)

    def head_pair(hpair, mx_even, last):
        mx_odd = scores(2 * hpair + 1, 1)
        out_even = values(hpair, 0, mx_even)
        mx_next = mx_even if last else scores(2 * hpair + 2, 0)
        out_odd = values(hpair, 1, mx_odd)
        oh_ref[hpair] = jnp.where(even_rows, out_even, out_odd).T.astype(BF16)
        return mx_next

    n_pairs = MLA_HEADS // 2
    mx0 = lax.fori_loop(0, n_pairs - 1, lambda i, mx: head_pair(i, mx, False), scores(0, 0))
    head_pair(n_pairs - 1, mx0, True)
    heads = jnp.concatenate([oh_ref[i] for i in range(n_pairs)], axis=1)
    o_ref[0] = x_ref[0] + g1_ref[0] * _dot(heads, wo_ref[...])


def _mla_attn(q, k_ctx, k_lat, v_ctx, v_lat, x, g1, w_o, tq):
    n, L, d = x.shape
    n_ctx = k_ctx.shape[2]
    nh, nhp = MLA_HEADS, MLA_HEADS // 2
    samp = lambda b, t: (b, 0, 0, 0)
    row = lambda b, t: (b, t, 0)
    assert L % KEY_CHUNK == 0
    return pl.pallas_call(
        functools.partial(_mla_attn_kernel, n_ctx=n_ctx, n_lat=L),
        grid=(n, L // tq),
        in_specs=[
            pl.BlockSpec((1, nh, HEAD_PAD, tq), lambda b, t: (b, 0, 0, t)),
            pl.BlockSpec((1, nh, n_ctx, HEAD_PAD), samp),
            pl.BlockSpec((1, nh, L, HEAD_PAD), samp),
            pl.BlockSpec((1, nhp, V_ROWS, n_ctx), samp),
            pl.BlockSpec((1, nhp, V_ROWS, L), samp, pipeline_mode=pl.Buffered(1)),
            pl.BlockSpec((1, tq, d), row),
            _mod_spec(True, d),
            pl.BlockSpec(w_o.shape, lambda b, t: (0, 0), pipeline_mode=pl.Buffered(1)),
        ],
        out_specs=pl.BlockSpec((1, tq, d), row),
        out_shape=jax.ShapeDtypeStruct((n, L, d), F32),
        scratch_shapes=[pltpu.VMEM((nhp, tq, 2 * V_HEAD), BF16),
                        pltpu.VMEM((2, n_ctx + L, tq), F32)],
        compiler_params=_cparams("arbitrary", "arbitrary"),
        name="mla_attn",
    )(q, k_ctx, k_lat, v_ctx, v_lat, x, g1, w_o)


def _route_kernel(x_ref, g_ref, sh_ref, sc_ref, wr_ref, u_ref, slot_ref, aff_ref, slott_ref, first_ref,
                  afft_ref, tri_ref, *, L, tl, cap, gs):
    s = pl.program_id(1)
    t = pl.program_id(2)
    ne = N_EXPERTS
    rows = gs * ne

    @pl.when((pl.program_id(0) == 0) & (s == 0) & (t == 0))
    def _():
        r = lax.broadcasted_iota(jnp.int32, (PREFIX_BLOCK, PREFIX_BLOCK), 0)
        c = lax.broadcasted_iota(jnp.int32, (PREFIX_BLOCK, PREFIX_BLOCK), 1)
        tri_ref[...] = jnp.where(r < c, 1.0, 0.0).astype(BF16)

    u = _norm_modulate(x_ref[0], g_ref[...], sh_ref[0], sc_ref[0]).astype(BF16)
    u_ref[0] = u
    logits = _dot_nt(wr_ref[...], u)
    e = jnp.exp(logits - jnp.max(logits, axis=0, keepdims=True))
    t0 = pl.multiple_of(t * tl, tl)
    afft_ref[pl.ds(pl.multiple_of(s * ne, ne), ne), pl.ds(t0, tl)] = e / jnp.sum(e, axis=0, keepdims=True)

    @pl.when((s == gs - 1) & (t == pl.num_programs(2) - 1))
    def _():
        aff = afft_ref[...]

        def as_f32(bits):
            return lax.bitcast_convert_type(bits, F32)

        def count_ge(th):
            return jnp.sum(jnp.where(aff >= th, 1.0, 0.0), axis=1, keepdims=True)

        def body(_, carry):
            lo, hi = carry
            mid = lo + ((hi - lo) >> 1)
            ok = count_ge(as_f32(mid)) >= cap
            return jnp.where(ok, mid, lo), jnp.where(ok, hi, mid)

        lo0 = jnp.zeros((rows, 1), jnp.int32)
        hi0 = jnp.full((rows, 1), 0x7F800000, jnp.int32)
        lo_bits, hi_bits = lax.fori_loop(0, 31, body, (lo0, hi0))
        gt = aff >= as_f32(hi_bits)
        eq = (aff >= as_f32(lo_bits)) & jnp.logical_not(gt)
        need = cap - jnp.sum(jnp.where(gt, 1.0, 0.0), axis=1, keepdims=True)

        def prefix_count(mask):
            ones = jnp.where(mask, 1.0, 0.0)
            base = jnp.zeros((rows, 1), F32)
            parts = []
            for j in range(L // PREFIX_BLOCK):
                blk = ones[:, j * PREFIX_BLOCK:(j + 1) * PREFIX_BLOCK]
                parts.append(_dot(blk.astype(BF16), tri_ref[...]) + base)
                base = base + jnp.sum(blk, axis=1, keepdims=True)
            return parts[0] if len(parts) == 1 else jnp.concatenate(parts, axis=1)

        sel = gt | (eq & (prefix_count(eq) < need))
        pos = prefix_count(sel)
        tok = lax.broadcasted_iota(jnp.int32, (L, LANES), 0)
        blk_start = lax.broadcasted_iota(jnp.int32, (L, LANES), 1) * FIRST_BLOCK
        before = jnp.where(tok < blk_start, 1.0, 0.0).astype(BF16)
        first_i = _dot(jnp.where(sel, 1.0, 0.0).astype(BF16), before).astype(jnp.int32)
        slot = jnp.where(sel, pos, -1.0)
        slot_i = slot.astype(jnp.int32)
        fill = jnp.full((LANES - ne, L), -1.0, F32)
        for si in range(gs):
            first_ref[si] = first_i[si * ne:(si + 1) * ne]
            for ex in range(ne):
                r = si * ne + ex
                slot_ref[si, ex] = slot_i[r:r + 1, :]
                aff_ref[si, ex] = aff[r:r + 1, :]
            slott_ref[si] = jnp.concatenate([slot[si * ne:(si + 1) * ne], fill], axis=0).T


def _route(x, g, sh, sc, per_sample, w_router_t, cap, tl, gs):
    n, L, d = x.shape
    ne = N_EXPERTS
    assert n % gs == 0 and L % tl == 0
    vec = lambda gi, s, t: (0, 0)
    row = lambda gi, s, t: (gi * gs + s, t, 0)
    group4 = lambda gi, s, t: (gi, 0, 0, 0)
    if per_sample:
        mod_spec = pl.BlockSpec((1, 1, d), lambda gi, s, t: (gi * gs + s, 0, 0))
    else:
        mod_spec = pl.BlockSpec((1, 1, d), lambda gi, s, t: (0, 0, 0))
    return pl.pallas_call(
        functools.partial(_route_kernel, L=L, tl=tl, cap=cap, gs=gs),
        grid=(n // gs, gs, L // tl),
        in_specs=[
            pl.BlockSpec((1, tl, d), row),
            pl.BlockSpec((1, d), vec),
            mod_spec,
            mod_spec,
            pl.BlockSpec((ne, d), vec),
        ],
        out_specs=[
            pl.BlockSpec((1, tl, d), row),
            pl.BlockSpec((gs, ne, 1, L), group4),
            pl.BlockSpec((gs, ne, 1, L), group4),
            pl.BlockSpec((gs, L, LANES), lambda gi, s, t: (gi, 0, 0)),
            pl.BlockSpec((gs, ne, LANES), lambda gi, s, t: (gi, 0, 0)),
        ],
        out_shape=[
            jax.ShapeDtypeStruct((n, L, d), BF16),
            jax.ShapeDtypeStruct((n, ne, 1, L), jnp.int32),
            jax.ShapeDtypeStruct((n, ne, 1, L), F32),
            jax.ShapeDtypeStruct((n, L, LANES), F32),
            jax.ShapeDtypeStruct((n, ne, LANES), jnp.int32),
        ],
        scratch_shapes=[pltpu.VMEM((gs * ne, L), F32), pltpu.VMEM((PREFIX_BLOCK, PREFIX_BLOCK), BF16)],
        compiler_params=_cparams("arbitrary", "arbitrary", "arbitrary"),
        name="route",
    )(x, g, sh, sc, w_router_t)


def _expert_kernel(first_ref, u_ref, slot_ref, aff_ref, w1_ref, w3_ref, w2_ref, y_ref, w1b, w3b, w2b,
                   xg_ref, gate_ref, *, nb, cap):
    e = pl.program_id(0)
    bb = pl.program_id(1)

    @pl.when(bb == 0)
    def _():
        w1b[...] = w1_ref[0, 0].astype(BF16)
        w3b[...] = w3_ref[0, 0].astype(BF16)
        w2b[...] = w2_ref[0, 0].astype(BF16)

    L = u_ref.shape[1]
    nblk = L // PREFIX_BLOCK

    def gather_dense():
        slot_iota = lax.broadcasted_iota(jnp.int32, (cap, L), 0)
        for s in range(nb):
            hit = slot_ref[s, 0] == slot_iota
            xg_ref[s * cap:(s + 1) * cap, :] = _dot(jnp.where(hit, 1.0, 0.0).astype(BF16), u_ref[s])
            gate_ref[s * cap:(s + 1) * cap, :] = jnp.broadcast_to(
                jnp.sum(jnp.where(hit, aff_ref[s, 0], 0.0), axis=1, keepdims=True), (cap, LANES))

    if cap <= GATHER_ROWS:
        gather_dense()
    else:
        starts, fits = [], None
        for s in range(nb):
            base = ((e * pl.num_programs(1) + bb) * nb + s) * FIRST_STRIDE
            per = PREFIX_BLOCK // FIRST_BLOCK
            for j in range(nblk):
                start = (first_ref[base + per * j] // SUBLANES) * SUBLANES
                ok = first_ref[base + per * (j + 1)] - start <= GATHER_ROWS
                starts.append(start)
                fits = ok if fits is None else jnp.logical_and(fits, ok)

        @pl.when(fits)
        def _():
            xg_ref[...] = jnp.zeros(xg_ref.shape, F32)
            gate_ref[...] = jnp.zeros(gate_ref.shape, F32)
            row_iota = lax.broadcasted_iota(jnp.int32, (GATHER_ROWS, PREFIX_BLOCK), 0)
            for s in range(nb):
                for j in range(nblk):
                    toks = slice(j * PREFIX_BLOCK, (j + 1) * PREFIX_BLOCK)
                    start = pl.multiple_of(starts[s * nblk + j], SUBLANES)
                    hit = slot_ref[s, 0, :, toks] == (row_iota + start)
                    rows = pl.ds(s * cap + start, GATHER_ROWS)
                    xg_ref[rows, :] += _dot(jnp.where(hit, 1.0, 0.0).astype(BF16), u_ref[s, toks, :])
                    gate_ref[rows, :] += jnp.broadcast_to(
                        jnp.sum(jnp.where(hit, aff_ref[s, 0, :, toks], 0.0), axis=1, keepdims=True),
                        (GATHER_ROWS, LANES))

        @pl.when(jnp.logical_not(fits))
        def _():
            gather_dense()

    xg = xg_ref[0:nb * cap, :].astype(BF16)
    gate = gate_ref[0:nb * cap, 0:1]
    h1 = _dot(xg, w1b[...])
    hid = (h1 * _sigmoid(h1) * _dot(xg, w3b[...])).astype(BF16)
    y = (_dot(hid, w2b[...]) * gate).astype(BF16)
    for s in range(nb):
        y_ref[0, s] = y[s * cap:(s + 1) * cap]


def _experts(u, slot, aff, first, layer, w1, w3, w2, cap, nb):
    n, L, d = u.shape
    _, ne, _, ff = w1.shape
    assert L // FIRST_BLOCK < FIRST_STRIDE
    first_flat = jnp.transpose(first[:, :, :FIRST_STRIDE], (1, 0, 2)).reshape(-1)
    wspec = lambda shape: pl.BlockSpec((1, 1) + shape, lambda e, b, tab: (layer, e, 0, 0))
    rows = nb * cap + GATHER_ROWS
    grid_spec = pltpu.PrefetchScalarGridSpec(
        num_scalar_prefetch=1,
        grid=(ne, n // nb),
        in_specs=[
            pl.BlockSpec((nb, L, d), lambda e, b, tab: (b, 0, 0)),
            pl.BlockSpec((nb, 1, 1, L), lambda e, b, tab: (b, e, 0, 0)),
            pl.BlockSpec((nb, 1, 1, L), lambda e, b, tab: (b, e, 0, 0)),
            wspec((d, ff)), wspec((d, ff)), wspec((ff, d)),
        ],
        out_specs=pl.BlockSpec((1, nb, cap, d), lambda e, b, tab: (e, b, 0, 0)),
        scratch_shapes=[pltpu.VMEM((d, ff), BF16), pltpu.VMEM((d, ff), BF16), pltpu.VMEM((ff, d), BF16),
                        pltpu.VMEM((rows, d), F32), pltpu.VMEM((rows, LANES), F32)],
    )
    return pl.pallas_call(
        functools.partial(_expert_kernel, nb=nb, cap=cap),
        grid_spec=grid_spec,
        out_shape=jax.ShapeDtypeStruct((ne, n, cap, d), BF16),
        compiler_params=_cparams("arbitrary", "arbitrary"),
        name="experts",
    )(first_flat, u, slot, aff, w1, w3, w2)


def _combine_kernel(*refs, cap, final):
    if final:
        first_ref, y_ref, st_ref, x_ref, g2_ref, fg_ref, o_ref = refs
    else:
        first_ref, y_ref, st_ref, x_ref, g2_ref, o_ref = refs
    b = pl.program_id(0)
    t = pl.program_id(1)
    tl = x_ref.shape[1]
    d = x_ref.shape[2]
    ne = N_EXPERTS

    def residual(rows, moe):
        o_ref[0, rows, :] = x_ref[0, rows, :] + g2_ref[0] * moe

    def normalize():
        if final:
            o_ref[0] = _rms(o_ref[0], fg_ref[...])

    def scatter_dense():
        st = st_ref[0]
        lane = lax.broadcasted_iota(jnp.int32, (tl, cap), 1).astype(F32)
        onehot = jnp.concatenate(
            [jnp.where(st[:, e:e + 1] == lane, 1.0, 0.0).astype(BF16) for e in range(ne)], axis=1)
        residual(slice(0, tl), _dot(onehot, y_ref[...].reshape(ne * cap, d)))
        normalize()

    if cap <= SCATTER_ROWS:
        scatter_dense()
        return

    nsub = tl // FIRST_BLOCK
    starts, fits = [], None
    for q in range(nsub):
        for e in range(ne):
            entry = (b * ne + e) * FIRST_STRIDE + t * nsub + q
            start = jnp.minimum((first_ref[entry] // BF16_ROWS) * BF16_ROWS, cap - SCATTER_ROWS)
            ok = first_ref[entry + 1] - start <= SCATTER_ROWS
            starts.append(start)
            fits = ok if fits is None else jnp.logical_and(fits, ok)

    @pl.when(fits)
    def _():
        per = LANES // SCATTER_ROWS
        lane = lax.broadcasted_iota(jnp.int32, (FIRST_BLOCK, LANES), 1).astype(F32)
        for q in range(nsub):
            rows = slice(q * FIRST_BLOCK, (q + 1) * FIRST_BLOCK)
            st = st_ref[0, rows, :]
            hots, wins = [], []
            for grp in range(ne // per):
                group = [grp * per + k for k in range(per)]
                first = [pl.multiple_of(starts[q * ne + e], BF16_ROWS) for e in group]
                for e, start in zip(group, first):
                    wins.append(y_ref[e, 0, pl.ds(start, SCATTER_ROWS), :])
                want = None
                for k in reversed(range(per)):
                    e = group[k]
                    mine = st[:, e:e + 1] - (first[k] - k * SCATTER_ROWS).astype(F32)
                    want = mine if want is None else jnp.where(lane < (k + 1) * SCATTER_ROWS, mine, want)
                hots.append(jnp.where(want == lane, 1.0, 0.0).astype(BF16))
            residual(rows, _dot(jnp.concatenate(hots, axis=1), jnp.concatenate(wins, axis=0)))
        normalize()

    @pl.when(jnp.logical_not(fits))
    def _():
        scatter_dense()


def _combine(y, slot_t, first, x, g2, per_sample, cap, tl, final_g=None):
    n, L, d = x.shape
    ne = N_EXPERTS
    final = final_g is not None
    assert L // FIRST_BLOCK < FIRST_STRIDE and tl % FIRST_BLOCK == 0
    first_flat = first[:, :, :FIRST_STRIDE].reshape(-1)
    if per_sample:
        mod_spec = pl.BlockSpec((1, 1, d), lambda b, t, tab: (b, 0, 0))
    else:
        mod_spec = pl.BlockSpec((1, 1, d), lambda b, t, tab: (0, 0, 0))
    in_specs = [
        pl.BlockSpec((ne, 1, cap, d), lambda b, t, tab: (0, b, 0, 0)),
        pl.BlockSpec((1, tl, LANES), lambda b, t, tab: (b, t, 0)),
        pl.BlockSpec((1, tl, d), lambda b, t, tab: (b, t, 0)),
        mod_spec,
    ]
    args = [first_flat, y, slot_t, x, g2]
    if final:
        in_specs += [pl.BlockSpec((1, d), lambda b, t, tab: (0, 0))]
        args += [final_g.reshape(1, d)]
    grid_spec = pltpu.PrefetchScalarGridSpec(
        num_scalar_prefetch=1,
        grid=(n, L // tl),
        in_specs=in_specs,
        out_specs=pl.BlockSpec((1, tl, d), lambda b, t, tab: (b, t, 0)),
    )
    return pl.pallas_call(
        functools.partial(_combine_kernel, cap=cap, final=final),
        grid_spec=grid_spec,
        out_shape=jax.ShapeDtypeStruct((n, L, d), F32),
        compiler_params=_cparams("arbitrary", "arbitrary"),
        name="combine",
    )(*args)


def _ec_moe_layer(x, norm_g, sh, sc, g2, per_sample, layer, w_router, w1, w3, w2, tl, nb, gs, final_g=None):
    n, L, d = x.shape
    cap = EC_CAPACITY_FACTOR * L // N_EXPERTS
    u, slot, aff, slot_t, first = _route(x, norm_g.reshape(1, d), sh, sc, per_sample,
                                         w_router.T.astype(BF16), cap, tl, gs)
    y = _experts(u, slot, aff, first, layer, w1, w3, w2, cap, nb)
    return _combine(y, slot_t, first, x, g2, per_sample, cap, tl, final_g)


def kernel(x, c, ctx, c_ctx, mod_w, mod_b, norm1_g, norm2_g, ab_w_in, ab_conv_w, ab_conv_b, ab_ln_g, ab_ln_b, ab_w_out, mla_w_in, mla_q_norm_g, mla_kv_norm_g, mla_w_uq, mla_w_ukv, mla_w_o, moe_w_router, moe_w1, moe_w3, moe_w2, final_g):
    n, L, d = x.shape
    n_ctx = ctx.shape[1]
    depth = mod_w.shape[0]
    assert n < MOD_ROWS
    cvecs = jnp.concatenate([c, c_ctx[None, :], jnp.zeros((MOD_ROWS - n - 1, d), F32)], axis=0)
    mod = _modulation(cvecs, mod_w, mod_b)

    def mods(i, ctx_stream):
        rows = mod[i, n:n + 1] if ctx_stream else mod[i, :n]
        return [rows[:, k * d:(k + 1) * d].reshape(-1, 1, d) for k in range(N_MOD)]

    x_lat, x_ctx = x, ctx
    for i in range(depth):
        ctx_out = i < depth - 1
        last = i == depth - 1
        j = i // 2
        sh1, sc1, g1, sh2, sc2, g2 = mods(i, False)
        csh1, csc1, cg1, csh2, csc2, cg2 = mods(i, True)
        if i % 2 == 0:
            ab = (ab_w_in[j], ab_conv_w[j], ab_conv_b[j], ab_ln_g[j], ab_ln_b[j], ab_w_out[j])
            x_lat = _conv_fourier_layer(x_lat, norm1_g[i], sh1, sc1, g1, True, *ab, tl=512, tq=256)
            if ctx_out:
                x_ctx = _conv_fourier_layer(x_ctx, norm1_g[i], csh1, csc1, cg1, False, *ab,
                                            tl=n_ctx, tq=n_ctx)
        else:
            assert not ctx_out, "context-stream attention output is only needed for deeper stacks"
            w_in_ext, wq, wkv = _mla_weights(mla_w_in[j], mla_w_uq[j], mla_w_ukv[j])
            cos_t, sin_t = _rope_tables(L)
            g = norm1_g[i].reshape(1, d)
            q_g = mla_q_norm_g[j].reshape(1, Q_LORA)
            kv_g = mla_kv_norm_g[j].reshape(1, KV_LORA)
            q, k_lat, v_lat = _mla_proj(x_lat, g, sh1, sc1, True, w_in_ext, q_g, kv_g, wq, wkv,
                                        cos_t, sin_t, True, 512)
            k_ctx, v_ctx = _mla_proj(x_ctx, g, csh1, csc1, False,
                                     w_in_ext[:, :Q_LORA + KV_LORA + HEAD_PAD], None, kv_g, None, wkv,
                                     None, None, False, n_ctx)
            x_lat = _mla_attn(q, k_ctx, k_lat, v_ctx, v_lat, x_lat, g1, mla_w_o[j].astype(BF16), 512)
        moe = (i, moe_w_router[i], moe_w1, moe_w3, moe_w2)
        x_lat = _ec_moe_layer(x_lat, norm2_g[i], sh2, sc2, g2, True, *moe, tl=512, nb=2, gs=4,
                              final_g=final_g if last else None)
        if ctx_out:
            x_ctx = _ec_moe_layer(x_ctx, norm2_g[i], csh2, csc2, cg2, False, *moe, tl=n_ctx, nb=n, gs=n)
    return x_lat
```

```python
import functools

import numpy as np
import jax
import jax.numpy as jnp
from jax import lax
from jax.experimental import pallas as pl
from jax.experimental.pallas import tpu as pltpu

F32 = jnp.float32
BF16 = jnp.bfloat16

NORM_EPS = 1e-6
N_MOD = 6
GRID_W = 64
CONV_WIDTH = 31
CONV_HALO = 16
CONV_TAP_GROUPS = 4
FOURIER_GROUPS = 4
MLA_HEADS = 16
QK_NOPE = 64
QK_ROPE = 32
V_HEAD = 64
V_ROWS = 2 * V_HEAD
QK_HEAD = QK_NOPE + QK_ROPE
HEAD_PAD = 128
Q_LORA = 256
KV_LORA = 128
ROPE_BASE = 10000.0
SCORE_SCALE = float(QK_HEAD ** -0.5 * np.log2(np.e))
KEY_CHUNK = 512
N_EXPERTS = 16
PREFIX_BLOCK = 256
GATHER_ROWS = 128
SUBLANES = 8
FIRST_BLOCK = 128
FIRST_STRIDE = 32
SCATTER_ROWS = 64
BF16_ROWS = 16
EC_CAPACITY_FACTOR = 2
MOD_ROWS = 24
LANES = 128
VMEM_LIMIT = 56 * 1024 * 1024


def _cparams(*sem):
    return pltpu.CompilerParams(dimension_semantics=sem, vmem_limit_bytes=VMEM_LIMIT)


def _dot(a, b):
    return jnp.dot(a, b, preferred_element_type=F32)


def _dot_nt(a, b):
    return lax.dot_general(a, b, (((1,), (1,)), ((), ())), preferred_element_type=F32)


def _sigmoid(x):
    return 1.0 / (1.0 + jnp.exp(-x))


def _norm_modulate(x, g, shift, scale):
    ms = jnp.mean(x * x, axis=-1, keepdims=True)
    y = x * lax.rsqrt(ms + NORM_EPS) * g
    return y * (1.0 + scale) + shift


def _mod_kernel(c_ref, w_ref, b_ref, o_ref):
    c = c_ref[...]
    s = (c * _sigmoid(c)).astype(BF16)
    o_ref[0] = _dot(s, w_ref[0].astype(BF16)) + b_ref[0]


def _modulation(cvecs, mod_w, mod_b):
    depth, d, nd = mod_w.shape
    tn = 1024
    return pl.pallas_call(
        _mod_kernel,
        grid=(depth, nd // tn),
        in_specs=[
            pl.BlockSpec((MOD_ROWS, d), lambda i, j: (0, 0)),
            pl.BlockSpec((1, d, tn), lambda i, j: (i, 0, j)),
            pl.BlockSpec((1, 1, tn), lambda i, j: (i, 0, j)),
        ],
        out_specs=pl.BlockSpec((1, MOD_ROWS, tn), lambda i, j: (i, 0, j)),
        out_shape=jax.ShapeDtypeStruct((depth, MOD_ROWS, nd), F32),
        compiler_params=_cparams("arbitrary", "arbitrary"),
        name="modulation",
    )(cvecs, mod_w, mod_b.reshape(depth, 1, nd))


def _mod_spec(per_sample, d):
    if per_sample:
        return pl.BlockSpec((1, 1, d), lambda b, t: (b, 0, 0))
    return pl.BlockSpec((1, 1, d), lambda b, t: (0, 0, 0))


def _ab_in_kernel(x_ref, g_ref, sh_ref, sc_ref, w_ref, cs_ref, a_ref, y_ref, *, conv_ch, group_ch):
    u = _norm_modulate(x_ref[0], g_ref[...], sh_ref[0], sc_ref[0]).astype(BF16)
    z = _dot(u, w_ref[...])
    a_ref[0] = z[:, :conv_ch] * _sigmoid(z[:, conv_ch:2 * conv_ch])
    uf = z[:, 2 * conv_ch:].astype(BF16)
    for g in range(FOURIER_GROUPS):
        yg = _dot(uf[:, g * group_ch:(g + 1) * group_ch], cs_ref[...])
        y_ref[0, 0, :, g * group_ch:(g + 1) * group_ch] = yg[:, :group_ch].astype(BF16)
        y_ref[0, 1, :, g * group_ch:(g + 1) * group_ch] = yg[:, group_ch:].astype(BF16)


def _ab_in(x, g, sh, sc, per_sample, w_in, cs, conv_ch, tl):
    n, L, d = x.shape
    ab_in = w_in.shape[1]
    four_ch = ab_in - 2 * conv_ch
    group_ch = four_ch // FOURIER_GROUPS
    kern = functools.partial(_ab_in_kernel, conv_ch=conv_ch, group_ch=group_ch)
    return pl.pallas_call(
        kern,
        grid=(n, L // tl),
        in_specs=[
            pl.BlockSpec((1, tl, d), lambda b, t: (b, t, 0)),
            pl.BlockSpec((1, d), lambda b, t: (0, 0)),
            _mod_spec(per_sample, d),
            _mod_spec(per_sample, d),
            pl.BlockSpec((d, ab_in), lambda b, t: (0, 0)),
            pl.BlockSpec(cs.shape, lambda b, t: (0, 0)),
        ],
        out_specs=[
            pl.BlockSpec((1, tl, conv_ch), lambda b, t: (b, t, 0)),
            pl.BlockSpec((1, 2, tl, four_ch), lambda b, t: (b, 0, t, 0)),
        ],
        out_shape=[
            jax.ShapeDtypeStruct((n, L, conv_ch), F32),
            jax.ShapeDtypeStruct((n, 2, L, four_ch), BF16),
        ],
        compiler_params=_cparams("arbitrary", "arbitrary"),
        name="ab_in",
    )(x, g, sh, sc, w_in, cs)


def _ab_out_kernel(a_ref, y_ref, d_ref, x_ref, g1_ref, cw_ref, cb_ref, lg_ref, lb_ref, wo_ref,
                   o_ref, apad_ref, act_ref, *, L, tq, conv_ch):
    t = pl.program_id(1)

    @pl.when(t == 0)
    def _():
        apad_ref[0:CONV_HALO, :] = jnp.zeros((CONV_HALO, conv_ch), F32)
        apad_ref[CONV_HALO:CONV_HALO + L, :] = a_ref[0]
        apad_ref[CONV_HALO + L:CONV_HALO + L + CONV_HALO, :] = jnp.zeros((CONV_HALO, conv_ch), F32)

    t0 = pl.multiple_of(t * tq, tq)
    win = tq + 8 * CONV_TAP_GROUPS
    for c in range(conv_ch // LANES):
        cols = slice(c * LANES, (c + 1) * LANES)
        full = apad_ref[pl.ds(t0, win), cols]
        acc = jnp.zeros((tq, LANES), F32) + cb_ref[:, cols]
        for r in range(8):
            shifted = full if r == 0 else pltpu.roll(full, win - r, axis=0)
            for j in range(CONV_TAP_GROUPS):
                m = 8 * j + r
                if m == 0:
                    continue
                acc = acc + shifted[8 * j:8 * j + tq, :] * cw_ref[m:m + 1, cols]
        act_ref[:, cols] = acc
    four = _dot(d_ref[...], y_ref[0])
    conv = act_ref[...]
    mu = jnp.mean(conv, axis=-1, keepdims=True)
    cen = conv - mu
    var = jnp.mean(cen * cen, axis=-1, keepdims=True)
    ln = cen * lax.rsqrt(var + NORM_EPS) * lg_ref[...] + lb_ref[...]
    act = (ln * _sigmoid(ln)).astype(BF16)
    mix = _dot(act, wo_ref[0:conv_ch, :]) + _dot(four.astype(BF16), wo_ref[conv_ch:, :])
    o_ref[0] = x_ref[0] + g1_ref[0] * mix


def _ab_out(a, ystack, dmat, x, g1, per_sample, conv_w, conv_b, ln_g, ln_b, w_out, tq):
    n, L, d = x.shape
    conv_ch = a.shape[-1]
    four_ch = ystack.shape[-1]
    kern = functools.partial(_ab_out_kernel, L=L, tq=tq, conv_ch=conv_ch)
    vec = lambda b, t: (0, 0)
    return pl.pallas_call(
        kern,
        grid=(n, L // tq),
        in_specs=[
            pl.BlockSpec((1, L, conv_ch), lambda b, t: (b, 0, 0)),
            pl.BlockSpec((1, 2 * L, four_ch), lambda b, t: (b, 0, 0)),
            pl.BlockSpec((tq, 2 * L), lambda b, t: (t, 0)),
            pl.BlockSpec((1, tq, d), lambda b, t: (b, t, 0)),
            _mod_spec(per_sample, d),
            pl.BlockSpec(conv_w.shape, vec),
            pl.BlockSpec((1, conv_ch), vec),
            pl.BlockSpec((1, conv_ch), vec),
            pl.BlockSpec((1, conv_ch), vec),
            pl.BlockSpec(w_out.shape, vec),
        ],
        out_specs=pl.BlockSpec((1, tq, d), lambda b, t: (b, t, 0)),
        out_shape=jax.ShapeDtypeStruct((n, L, d), F32),
        scratch_shapes=[
            pltpu.VMEM((L + 2 * CONV_HALO, conv_ch), F32),
            pltpu.VMEM((tq, conv_ch), F32),
        ],
        compiler_params=_cparams("arbitrary", "arbitrary"),
        name="ab_out",
    )(a, ystack, dmat, x, g1, conv_w, conv_b, ln_g, ln_b, w_out)


def _dft_tables(L, group_ch):
    k = np.arange(L, dtype=np.int64)
    ang = 2.0 * np.pi * ((k[:, None] * k[None, :]) % L).astype(np.float64) / L
    dmat = np.concatenate([np.cos(ang), -np.sin(ang)], axis=1) / np.sqrt(L)
    m = np.arange(group_ch, dtype=np.int64)
    angc = 2.0 * np.pi * ((m[:, None] * m[None, :]) % group_ch).astype(np.float64) / group_ch
    cs = np.concatenate([np.cos(angc), np.sin(angc)], axis=1) / np.sqrt(group_ch)
    return jnp.asarray(dmat.astype(np.float32)).astype(BF16), jnp.asarray(cs.astype(np.float32)).astype(BF16)


def _conv_fourier_layer(x, norm_g, sh, sc, g1, per_sample, w_in, conv_w, conv_b, ln_g, ln_b, w_out, tl, tq):
    n, L, d = x.shape
    conv_ch = conv_w.shape[-1]
    four_ch = w_in.shape[1] - 2 * conv_ch
    dmat, cs = _dft_tables(L, four_ch // FOURIER_GROUPS)
    a, y = _ab_in(x, norm_g.reshape(1, d), sh, sc, per_sample, w_in.astype(BF16), cs, conv_ch, tl)
    cw = jnp.concatenate([jnp.zeros((1, conv_ch), F32), conv_w], axis=0)
    return _ab_out(a, y.reshape(n, 2 * L, four_ch), dmat, x, g1, per_sample, cw,
                   conv_b.reshape(1, conv_ch), ln_g.reshape(1, conv_ch), ln_b.reshape(1, conv_ch),
                   w_out.astype(BF16), tq)


def _rope_tables(L):
    rows = L // GRID_W
    row = np.repeat(np.arange(rows, dtype=np.float32), GRID_W)
    col = np.tile(np.arange(GRID_W, dtype=np.float32), rows)
    nf = QK_ROPE // 4
    inv_freq = (np.float32(ROPE_BASE) ** (-np.arange(nf, dtype=np.float32) / np.float32(nf))).astype(np.float32)
    ang = np.concatenate([row[:, None] * inv_freq, col[:, None] * inv_freq], axis=-1).astype(np.float32)
    cos = np.repeat(np.cos(ang.astype(np.float64)), 2, axis=-1)
    sin = np.repeat(np.sin(ang.astype(np.float64)), 2, axis=-1)
    cos_t = np.zeros((L, HEAD_PAD), np.float32)
    sin_t = np.zeros((L, HEAD_PAD), np.float32)
    cos_t[:, :QK_NOPE] = 1.0
    cos_t[:, QK_NOPE:QK_HEAD] = cos
    sin_t[:, QK_NOPE:QK_HEAD] = sin
    return jnp.asarray(cos_t), jnp.asarray(sin_t)


def _pair_swap(w):
    w2 = w.reshape(w.shape[:-1] + (QK_ROPE // 2, 2))
    return jnp.stack([-w2[..., 1], w2[..., 0]], axis=-1).reshape(w.shape)


def _mla_weights(w_in, w_uq, w_ukv):
    d = w_in.shape[0]
    h = MLA_HEADS
    zeros = lambda *s: jnp.zeros(s, F32)
    w_kr = w_in[:, Q_LORA + KV_LORA:]
    kr_a = jnp.concatenate([zeros(d, QK_NOPE), w_kr, zeros(d, HEAD_PAD - QK_HEAD)], axis=1)
    kr_b = jnp.concatenate([zeros(d, QK_NOPE), _pair_swap(w_kr), zeros(d, HEAD_PAD - QK_HEAD)], axis=1)
    w_in_ext = jnp.concatenate([w_in[:, :Q_LORA + KV_LORA], kr_a, kr_b], axis=1)
    wq = w_uq.reshape(Q_LORA, h, QK_HEAD)
    pad = zeros(Q_LORA, h, HEAD_PAD - QK_HEAD)
    wq_a = jnp.concatenate([wq, pad], axis=-1).reshape(Q_LORA, h * HEAD_PAD)
    wq_b = jnp.concatenate([zeros(Q_LORA, h, QK_NOPE), _pair_swap(wq[..., QK_NOPE:]), pad],
                           axis=-1).reshape(Q_LORA, h * HEAD_PAD)
    wkv = w_ukv.reshape(KV_LORA, h, QK_NOPE + V_HEAD)
    wk = jnp.concatenate([wkv[..., :QK_NOPE], zeros(KV_LORA, h, HEAD_PAD - QK_NOPE)],
                         axis=-1).reshape(KV_LORA, h * HEAD_PAD)
    wv = wkv[..., QK_NOPE:].reshape(KV_LORA, h * V_HEAD)
    return (w_in_ext.astype(BF16), jnp.concatenate([wq_a, wq_b], axis=1).astype(BF16),
            jnp.concatenate([wk, wv], axis=1).astype(BF16))


def _rms(x, g):
    return x * lax.rsqrt(jnp.mean(x * x, axis=-1, keepdims=True) + NORM_EPS) * g


def _mla_proj_kernel(*refs, latent):
    if latent:
        (x_ref, g_ref, sh_ref, sc_ref, win_ref, qg_ref, kvg_ref, wq_ref, wkv_ref, cos_ref, sin_ref,
         q_ref, k_ref, v_ref) = refs
    else:
        (x_ref, g_ref, sh_ref, sc_ref, win_ref, kvg_ref, wkv_ref, k_ref, v_ref) = refs
    hp = MLA_HEADS * HEAD_PAD
    u = _norm_modulate(x_ref[0], g_ref[...], sh_ref[0], sc_ref[0]).astype(BF16)
    z = _dot(u, win_ref[...])
    ckv = _rms(z[:, Q_LORA:Q_LORA + KV_LORA], kvg_ref[...]).astype(BF16)
    kr = z[:, Q_LORA + KV_LORA:Q_LORA + KV_LORA + HEAD_PAD]
    if latent:
        cos = cos_ref[...]
        sin = sin_ref[...]
        kr = kr * cos + z[:, Q_LORA + KV_LORA + HEAD_PAD:] * sin
        cq = _rms(z[:, :Q_LORA], qg_ref[...]).astype(BF16)
    pair_w = 2 * HEAD_PAD
    for hpair in range(MLA_HEADS // 2):
        pcols = slice(hpair * pair_w, (hpair + 1) * pair_w)
        if latent:
            qa = _dot(cq, wq_ref[:, pcols])
            qb = _dot(cq, wq_ref[:, hp + hpair * pair_w:hp + (hpair + 1) * pair_w])
        kn = _dot(ckv, wkv_ref[:, pcols])
        for par in range(2):
            h = 2 * hpair + par
            cols = slice(par * HEAD_PAD, (par + 1) * HEAD_PAD)
            if latent:
                q_ref[0, h] = ((qa[:, cols] * cos + qb[:, cols] * sin) * SCORE_SCALE).T.astype(BF16)
            k_ref[0, h] = (kn[:, cols] + kr).astype(BF16)
    vv = _dot(ckv, wkv_ref[:, hp:])
    for hpair in range(MLA_HEADS // 2):
        v_ref[0, hpair] = vv[:, hpair * 2 * V_HEAD:(hpair + 1) * 2 * V_HEAD].T.astype(BF16)


def _mla_proj(x, g, sh, sc, per_sample, w_in_ext, q_g, kv_g, wq, wkv, cos_t, sin_t, latent, tl):
    n, L, d = x.shape
    vec = lambda b, t: (0, 0)
    row = lambda b, t: (b, t, 0)
    in_specs = [
        pl.BlockSpec((1, tl, d), row),
        pl.BlockSpec((1, d), vec),
        _mod_spec(per_sample, d),
        _mod_spec(per_sample, d),
        pl.BlockSpec(w_in_ext.shape, vec),
    ]
    args = [x, g, sh, sc, w_in_ext]
    if latent:
        in_specs += [pl.BlockSpec((1, Q_LORA), vec)]
        args += [q_g]
    in_specs += [pl.BlockSpec((1, KV_LORA), vec)]
    args += [kv_g]
    if latent:
        in_specs += [pl.BlockSpec(wq.shape, vec)]
        args += [wq]
    in_specs += [pl.BlockSpec(wkv.shape, vec)]
    args += [wkv]
    nh, nhp = MLA_HEADS, MLA_HEADS // 2
    head_row = lambda b, t: (b, 0, t, 0)
    head_col = lambda b, t: (b, 0, 0, t)
    out_specs = [pl.BlockSpec((1, nh, tl, HEAD_PAD), head_row), pl.BlockSpec((1, nhp, V_ROWS, tl), head_col)]
    out_shape = [jax.ShapeDtypeStruct((n, nh, L, HEAD_PAD), BF16),
                 jax.ShapeDtypeStruct((n, nhp, V_ROWS, L), BF16)]
    if latent:
        in_specs += [pl.BlockSpec((tl, HEAD_PAD), lambda b, t: (t, 0))] * 2
        args += [cos_t, sin_t]
        out_specs = [pl.BlockSpec((1, nh, HEAD_PAD, tl), head_col)] + out_specs
        out_shape = [jax.ShapeDtypeStruct((n, nh, HEAD_PAD, L), BF16)] + out_shape
    return pl.pallas_call(
        functools.partial(_mla_proj_kernel, latent=latent),
        grid=(n, L // tl),
        in_specs=in_specs,
        out_specs=out_specs,
        out_shape=out_shape,
        compiler_params=_cparams("arbitrary", "arbitrary"),
        name="mla_proj_lat" if latent else "mla_proj_ctx",
    )(*args)


def _mla_attn_kernel(q_ref, kc_ref, kl_ref, vc_ref, vl_ref, x_ref, g1_ref, wo_ref, o_ref, oh_ref, s_ref,
                     *, n_ctx, n_lat):
    tq = q_ref.shape[3]
    chunks = [(True, 0, n_ctx, 0)]
    chunks += [(False, k0, KEY_CHUNK, n_ctx + k0) for k0 in range(0, n_lat, KEY_CHUNK)]
    even_rows = lax.broadcasted_iota(jnp.int32, (2 * V_HEAD, tq), 0) < V_HEAD

    def scores(h, par):
        qt = q_ref[0, h]
        mx = jnp.full((1, tq), -jnp.inf, F32)
        for is_ctx, k0, kn, r0 in chunks:
            k_ref = kc_ref if is_ctx else kl_ref
            s = _dot(k_ref[0, h, k0:k0 + kn, :], qt)
            s_ref[par, r0:r0 + kn, :] = s
            mx = jnp.maximum(mx, jnp.max(s, axis=0, keepdims=True))
        return mx

    def values(hpair, par, mx):
        den = jnp.zeros((1, tq), F32)
        acc = jnp.zeros((V_ROWS, tq), F32)
        for is_ctx, k0, kn, r0 in chunks:
            v_ref = vc_ref if is_ctx else vl_ref
            p = jnp.exp2(s_ref[par, r0:r0 + kn, :] - mx)
            den = den + jnp.sum(p, axis=0, keepdims=True)
            acc = acc + _dot(v_ref[0, hpair, :, k0:k0 + kn], p.astype(BF16))
        return acc * (1.0 / den)

    def head_pair(hpair, mx_even, last):
        mx_odd = scores(2 * hpair + 1, 1)
        out_even = values(hpair, 0, mx_even)
        mx_next = mx_even if last else scores(2 * hpair + 2, 0)
        out_odd = values(hpair, 1, mx_odd)
        oh_ref[hpair] = jnp.where(even_rows, out_even, out_odd).T.astype(BF16)
        return mx_next

    n_pairs = MLA_HEADS // 2
    mx0 = lax.fori_loop(0, n_pairs - 1, lambda i, mx: head_pair(i, mx, False), scores(0, 0))
    head_pair(n_pairs - 1, mx0, True)
    heads = jnp.concatenate([oh_ref[i] for i in range(n_pairs)], axis=1)
    o_ref[0] = x_ref[0] + g1_ref[0] * _dot(heads, wo_ref[...])


def _mla_attn(q, k_ctx, k_lat, v_ctx, v_lat, x, g1, w_o, tq):
    n, L, d = x.shape
    n_ctx = k_ctx.shape[2]
    nh, nhp = MLA_HEADS, MLA_HEADS // 2
    samp = lambda b, t: (b, 0, 0, 0)
    row = lambda b, t: (b, t, 0)
    assert L % KEY_CHUNK == 0
    return pl.pallas_call(
        functools.partial(_mla_attn_kernel, n_ctx=n_ctx, n_lat=L),
        grid=(n, L // tq),
        in_specs=[
            pl.BlockSpec((1, nh, HEAD_PAD, tq), lambda b, t: (b, 0, 0, t)),
            pl.BlockSpec((1, nh, n_ctx, HEAD_PAD), samp),
            pl.BlockSpec((1, nh, L, HEAD_PAD), samp),
            pl.BlockSpec((1, nhp, V_ROWS, n_ctx), samp),
            pl.BlockSpec((1, nhp, V_ROWS, L), samp, pipeline_mode=pl.Buffered(1)),
            pl.BlockSpec((1, tq, d), row),
            _mod_spec(True, d),
            pl.BlockSpec(w_o.shape, lambda b, t: (0, 0), pipeline_mode=pl.Buffered(1)),
        ],
        out_specs=pl.BlockSpec((1, tq, d), row),
        out_shape=jax.ShapeDtypeStruct((n, L, d), F32),
        scratch_shapes=[pltpu.VMEM((nhp, tq, 2 * V_HEAD), BF16),
                        pltpu.VMEM((2, n_ctx + L, tq), F32)],
        compiler_params=_cparams("arbitrary", "arbitrary"),
        name="mla_attn",
    )(q, k_ctx, k_lat, v_ctx, v_lat, x, g1, w_o)


def _route_kernel(x_ref, g_ref, sh_ref, sc_ref, wr_ref, u_ref, slot_ref, aff_ref, slott_ref, first_ref,
                  afft_ref, tri_ref, *, L, tl, cap, gs):
    s = pl.program_id(1)
    t = pl.program_id(2)
    ne = N_EXPERTS
    rows = gs * ne

    @pl.when((pl.program_id(0) == 0) & (s == 0) & (t == 0))
    def _():
        r = lax.broadcasted_iota(jnp.int32, (PREFIX_BLOCK, PREFIX_BLOCK), 0)
        c = lax.broadcasted_iota(jnp.int32, (PREFIX_BLOCK, PREFIX_BLOCK), 1)
        tri_ref[...] = jnp.where(r < c, 1.0, 0.0).astype(BF16)

    u = _norm_modulate(x_ref[0], g_ref[...], sh_ref[0], sc_ref[0]).astype(BF16)
    u_ref[0] = u
    logits = _dot_nt(wr_ref[...], u)
    e = jnp.exp(logits - jnp.max(logits, axis=0, keepdims=True))
    t0 = pl.multiple_of(t * tl, tl)
    afft_ref[pl.ds(pl.multiple_of(s * ne, ne), ne), pl.ds(t0, tl)] = e / jnp.sum(e, axis=0, keepdims=True)

    @pl.when((s == gs - 1) & (t == pl.num_programs(2) - 1))
    def _():
        aff = afft_ref[...]

        def as_f32(bits):
            return lax.bitcast_convert_type(bits, F32)

        def count_ge(th):
            return jnp.sum(jnp.where(aff >= th, 1.0, 0.0), axis=1, keepdims=True)

        def body(_, carry):
            lo, hi = carry
            mid = lo + ((hi - lo) >> 1)
            ok = count_ge(as_f32(mid)) >= cap
            return jnp.where(ok, mid, lo), jnp.where(ok, hi, mid)

        lo0 = jnp.zeros((rows, 1), jnp.int32)
        hi0 = jnp.full((rows, 1), 0x7F800000, jnp.int32)
        lo_bits, hi_bits = lax.fori_loop(0, 31, body, (lo0, hi0))
        gt = aff >= as_f32(hi_bits)
        eq = (aff >= as_f32(lo_bits)) & jnp.logical_not(gt)
        need = cap - jnp.sum(jnp.where(gt, 1.0, 0.0), axis=1, keepdims=True)

        def prefix_count(mask):
            ones = jnp.where(mask, 1.0, 0.0)
            base = jnp.zeros((rows, 1), F32)
            parts = []
            for j in range(L // PREFIX_BLOCK):
                blk = ones[:, j * PREFIX_BLOCK:(j + 1) * PREFIX_BLOCK]
                parts.append(_dot(blk.astype(BF16), tri_ref[...]) + base)
                base = base + jnp.sum(blk, axis=1, keepdims=True)
            return parts[0] if len(parts) == 1 else jnp.concatenate(parts, axis=1)

        sel = gt | (eq & (prefix_count(eq) < need))
        pos = prefix_count(sel)
        tok = lax.broadcasted_iota(jnp.int32, (L, LANES), 0)
        blk_start = lax.broadcasted_iota(jnp.int32, (L, LANES), 1) * FIRST_BLOCK
        before = jnp.where(tok < blk_start, 1.0, 0.0).astype(BF16)
        first_i = _dot(jnp.where(sel, 1.0, 0.0).astype(BF16), before).astype(jnp.int32)
        slot = jnp.where(sel, pos, -1.0)
        slot_i = slot.astype(jnp.int32)
        fill = jnp.full((LANES - ne, L), -1.0, F32)
        for si in range(gs):
            first_ref[si] = first_i[si * ne:(si + 1) * ne]
            for ex in range(ne):
                r = si * ne + ex
                slot_ref[si, ex] = slot_i[r:r + 1, :]
                aff_ref[si, ex] = aff[r:r + 1, :]
            slott_ref[si] = jnp.concatenate([slot[si * ne:(si + 1) * ne], fill], axis=0).T


def _route(x, g, sh, sc, per_sample, w_router_t, cap, tl, gs):
    n, L, d = x.shape
    ne = N_EXPERTS
    assert n % gs == 0 and L % tl == 0
    vec = lambda gi, s, t: (0, 0)
    row = lambda gi, s, t: (gi * gs + s, t, 0)
    group4 = lambda gi, s, t: (gi, 0, 0, 0)
    if per_sample:
        mod_spec = pl.BlockSpec((1, 1, d), lambda gi, s, t: (gi * gs + s, 0, 0))
    else:
        mod_spec = pl.BlockSpec((1, 1, d), lambda gi, s, t: (0, 0, 0))
    return pl.pallas_call(
        functools.partial(_route_kernel, L=L, tl=tl, cap=cap, gs=gs),
        grid=(n // gs, gs, L // tl),
        in_specs=[
            pl.BlockSpec((1, tl, d), row),
            pl.BlockSpec((1, d), vec),
            mod_spec,
            mod_spec,
            pl.BlockSpec((ne, d), vec),
        ],
        out_specs=[
            pl.BlockSpec((1, tl, d), row),
            pl.BlockSpec((gs, ne, 1, L), group4),
            pl.BlockSpec((gs, ne, 1, L), group4),
            pl.BlockSpec((gs, L, LANES), lambda gi, s, t: (gi, 0, 0)),
            pl.BlockSpec((gs, ne, LANES), lambda gi, s, t: (gi, 0, 0)),
        ],
        out_shape=[
            jax.ShapeDtypeStruct((n, L, d), BF16),
            jax.ShapeDtypeStruct((n, ne, 1, L), jnp.int32),
            jax.ShapeDtypeStruct((n, ne, 1, L), F32),
            jax.ShapeDtypeStruct((n, L, LANES), F32),
            jax.ShapeDtypeStruct((n, ne, LANES), jnp.int32),
        ],
        scratch_shapes=[pltpu.VMEM((gs * ne, L), F32), pltpu.VMEM((PREFIX_BLOCK, PREFIX_BLOCK), BF16)],
        compiler_params=_cparams("arbitrary", "arbitrary", "arbitrary"),
        name="route",
    )(x, g, sh, sc, w_router_t)


def _expert_kernel(first_ref, u_ref, slot_ref, aff_ref, w1_ref, w3_ref, w2_ref, y_ref, w1b, w3b, w2b,
                   xg_ref, gate_ref, *, nb, cap):
    e = pl.program_id(0)
    bb = pl.program_id(1)

    @pl.when(bb == 0)
    def _():
        w1b[...] = w1_ref[0, 0].astype(BF16)
        w3b[...] = w3_ref[0, 0].astype(BF16)
        w2b[...] = w2_ref[0, 0].astype(BF16)

    L = u_ref.shape[1]
    nblk = L // PREFIX_BLOCK

    def gather_dense():
        slot_iota = lax.broadcasted_iota(jnp.int32, (cap, L), 0)
        for s in range(nb):
            hit = slot_ref[s, 0] == slot_iota
            xg_ref[s * cap:(s + 1) * cap, :] = _dot(jnp.where(hit, 1.0, 0.0).astype(BF16), u_ref[s])
            gate_ref[s * cap:(s + 1) * cap, :] = jnp.broadcast_to(
                jnp.sum(jnp.where(hit, aff_ref[s, 0], 0.0), axis=1, keepdims=True), (cap, LANES))

    if cap <= GATHER_ROWS:
        gather_dense()
    else:
        starts, fits = [], None
        for s in range(nb):
            base = ((e * pl.num_programs(1) + bb) * nb + s) * FIRST_STRIDE
            per = PREFIX_BLOCK // FIRST_BLOCK
            for j in range(nblk):
                start = (first_ref[base + per * j] // SUBLANES) * SUBLANES
                ok = first_ref[base + per * (j + 1)] - start <= GATHER_ROWS
                starts.append(start)
                fits = ok if fits is None else jnp.logical_and(fits, ok)

        @pl.when(fits)
        def _():
            xg_ref[...] = jnp.zeros(xg_ref.shape, F32)
            gate_ref[...] = jnp.zeros(gate_ref.shape, F32)
            row_iota = lax.broadcasted_iota(jnp.int32, (GATHER_ROWS, PREFIX_BLOCK), 0)
            for s in range(nb):
                for j in range(nblk):
                    toks = slice(j * PREFIX_BLOCK, (j + 1) * PREFIX_BLOCK)
                    start = pl.multiple_of(starts[s * nblk + j], SUBLANES)
                    hit = slot_ref[s, 0, :, toks] == (row_iota + start)
                    rows = pl.ds(s * cap + start, GATHER_ROWS)
                    xg_ref[rows, :] += _dot(jnp.where(hit, 1.0, 0.0).astype(BF16), u_ref[s, toks, :])
                    gate_ref[rows, :] += jnp.broadcast_to(
                        jnp.sum(jnp.where(hit, aff_ref[s, 0, :, toks], 0.0), axis=1, keepdims=True),
                        (GATHER_ROWS, LANES))

        @pl.when(jnp.logical_not(fits))
        def _():
            gather_dense()

    xg = xg_ref[0:nb * cap, :].astype(BF16)
    gate = gate_ref[0:nb * cap, 0:1]
    h1 = _dot(xg, w1b[...])
    hid = (h1 * _sigmoid(h1) * _dot(xg, w3b[...])).astype(BF16)
    y = (_dot(hid, w2b[...]) * gate).astype(BF16)
    for s in range(nb):
        y_ref[0, s] = y[s * cap:(s + 1) * cap]


def _experts(u, slot, aff, first, layer, w1, w3, w2, cap, nb):
    n, L, d = u.shape
    _, ne, _, ff = w1.shape
    assert L // FIRST_BLOCK < FIRST_STRIDE
    first_flat = jnp.transpose(first[:, :, :FIRST_STRIDE], (1, 0, 2)).reshape(-1)
    wspec = lambda shape: pl.BlockSpec((1, 1) + shape, lambda e, b, tab: (layer, e, 0, 0))
    rows = nb * cap + GATHER_ROWS
    grid_spec = pltpu.PrefetchScalarGridSpec(
        num_scalar_prefetch=1,
        grid=(ne, n // nb),
        in_specs=[
            pl.BlockSpec((nb, L, d), lambda e, b, tab: (b, 0, 0)),
            pl.BlockSpec((nb, 1, 1, L), lambda e, b, tab: (b, e, 0, 0)),
            pl.BlockSpec((nb, 1, 1, L), lambda e, b, tab: (b, e, 0, 0)),
            wspec((d, ff)), wspec((d, ff)), wspec((ff, d)),
        ],
        out_specs=pl.BlockSpec((1, nb, cap, d), lambda e, b, tab: (e, b, 0, 0)),
        scratch_shapes=[pltpu.VMEM((d, ff), BF16), pltpu.VMEM((d, ff), BF16), pltpu.VMEM((ff, d), BF16),
                        pltpu.VMEM((rows, d), F32), pltpu.VMEM((rows, LANES), F32)],
    )
    return pl.pallas_call(
        functools.partial(_expert_kernel, nb=nb, cap=cap),
        grid_spec=grid_spec,
        out_shape=jax.ShapeDtypeStruct((ne, n, cap, d), BF16),
        compiler_params=_cparams("arbitrary", "arbitrary"),
        name="experts",
    )(first_flat, u, slot, aff, w1, w3, w2)


def _combine_kernel(*refs, cap, final):
    if final:
        first_ref, y_ref, st_ref, x_ref, g2_ref, fg_ref, o_ref = refs
    else:
        first_ref, y_ref, st_ref, x_ref, g2_ref, o_ref = refs
    b = pl.program_id(0)
    t = pl.program_id(1)
    tl = x_ref.shape[1]
    d = x_ref.shape[2]
    ne = N_EXPERTS

    def residual(rows, moe):
        o_ref[0, rows, :] = x_ref[0, rows, :] + g2_ref[0] * moe

    def normalize():
        if final:
            o_ref[0] = _rms(o_ref[0], fg_ref[...])

    def scatter_dense():
        st = st_ref[0]
        lane = lax.broadcasted_iota(jnp.int32, (tl, cap), 1).astype(F32)
        onehot = jnp.concatenate(
            [jnp.where(st[:, e:e + 1] == lane, 1.0, 0.0).astype(BF16) for e in range(ne)], axis=1)
        residual(slice(0, tl), _dot(onehot, y_ref[...].reshape(ne * cap, d)))
        normalize()

    if cap <= SCATTER_ROWS:
        scatter_dense()
        return

    nsub = tl // FIRST_BLOCK
    starts, fits = [], None
    for q in range(nsub):
        for e in range(ne):
            entry = (b * ne + e) * FIRST_STRIDE + t * nsub + q
            start = jnp.minimum((first_ref[entry] // BF16_ROWS) * BF16_ROWS, cap - SCATTER_ROWS)
            ok = first_ref[entry + 1] - start <= SCATTER_ROWS
            starts.append(start)
            fits = ok if fits is None else jnp.logical_and(fits, ok)

    @pl.when(fits)
    def _():
        per = LANES // SCATTER_ROWS
        lane = lax.broadcasted_iota(jnp.int32, (FIRST_BLOCK, LANES), 1).astype(F32)
        for q in range(nsub):
            rows = slice(q * FIRST_BLOCK, (q + 1) * FIRST_BLOCK)
            st = st_ref[0, rows, :]
            hots, wins = [], []
            for grp in range(ne // per):
                group = [grp * per + k for k in range(per)]
                first = [pl.multiple_of(starts[q * ne + e], BF16_ROWS) for e in group]
                for e, start in zip(group, first):
                    wins.append(y_ref[e, 0, pl.ds(start, SCATTER_ROWS), :])
                want = None
                for k in reversed(range(per)):
                    e = group[k]
                    mine = st[:, e:e + 1] - (first[k] - k * SCATTER_ROWS).astype(F32)
                    want = mine if want is None else jnp.where(lane < (k + 1) * SCATTER_ROWS, mine, want)
                hots.append(jnp.where(want == lane, 1.0, 0.0).astype(BF16))
            residual(rows, _dot(jnp.concatenate(hots, axis=1), jnp.concatenate(wins, axis=0)))
        normalize()

    @pl.when(jnp.logical_not(fits))
    def _():
        scatter_dense()


def _combine(y, slot_t, first, x, g2, per_sample, cap, tl, final_g=None):
    n, L, d = x.shape
    ne = N_EXPERTS
    final = final_g is not None
    assert L // FIRST_BLOCK < FIRST_STRIDE and tl % FIRST_BLOCK == 0
    first_flat = first[:, :, :FIRST_STRIDE].reshape(-1)
    if per_sample:
        mod_spec = pl.BlockSpec((1, 1, d), lambda b, t, tab: (b, 0, 0))
    else:
        mod_spec = pl.BlockSpec((1, 1, d), lambda b, t, tab: (0, 0, 0))
    in_specs = [
        pl.BlockSpec((ne, 1, cap, d), lambda b, t, tab: (0, b, 0, 0)),
        pl.BlockSpec((1, tl, LANES), lambda b, t, tab: (b, t, 0)),
        pl.BlockSpec((1, tl, d), lambda b, t, tab: (b, t, 0)),
        mod_spec,
    ]
    args = [first_flat, y, slot_t, x, g2]
    if final:
        in_specs += [pl.BlockSpec((1, d), lambda b, t, tab: (0, 0))]
        args += [final_g.reshape(1, d)]
    grid_spec = pltpu.PrefetchScalarGridSpec(
        num_scalar_prefetch=1,
        grid=(n, L // tl),
        in_specs=in_specs,
        out_specs=pl.BlockSpec((1, tl, d), lambda b, t, tab: (b, t, 0)),
    )
    return pl.pallas_call(
        functools.partial(_combine_kernel, cap=cap, final=final),
        grid_spec=grid_spec,
        out_shape=jax.ShapeDtypeStruct((n, L, d), F32),
        compiler_params=_cparams("arbitrary", "arbitrary"),
        name="combine",
    )(*args)


def _ec_moe_layer(x, norm_g, sh, sc, g2, per_sample, layer, w_router, w1, w3, w2, tl, nb, gs, final_g=None):
    n, L, d = x.shape
    cap = EC_CAPACITY_FACTOR * L // N_EXPERTS
    u, slot, aff, slot_t, first = _route(x, norm_g.reshape(1, d), sh, sc, per_sample,
                                         w_router.T.astype(BF16), cap, tl, gs)
    y = _experts(u, slot, aff, first, layer, w1, w3, w2, cap, nb)
    return _combine(y, slot_t, first, x, g2, per_sample, cap, tl, final_g)


def kernel(x, c, ctx, c_ctx, mod_w, mod_b, norm1_g, norm2_g, ab_w_in, ab_conv_w, ab_conv_b, ab_ln_g, ab_ln_b, ab_w_out, mla_w_in, mla_q_norm_g, mla_kv_norm_g, mla_w_uq, mla_w_ukv, mla_w_o, moe_w_router, moe_w1, moe_w3, moe_w2, final_g):
    n, L, d = x.shape
    n_ctx = ctx.shape[1]
    depth = mod_w.shape[0]
    assert n < MOD_ROWS
    cvecs = jnp.concatenate([c, c_ctx[None, :], jnp.zeros((MOD_ROWS - n - 1, d), F32)], axis=0)
    mod = _modulation(cvecs, mod_w, mod_b)

    def mods(i, ctx_stream):
        rows = mod[i, n:n + 1] if ctx_stream else mod[i, :n]
        return [rows[:, k * d:(k + 1) * d].reshape(-1, 1, d) for k in range(N_MOD)]

    x_lat, x_ctx = x, ctx
    for i in range(depth):
        ctx_out = i < depth - 1
        last = i == depth - 1
        j = i // 2
        sh1, sc1, g1, sh2, sc2, g2 = mods(i, False)
        csh1, csc1, cg1, csh2, csc2, cg2 = mods(i, True)
        if i % 2 == 0:
            ab = (ab_w_in[j], ab_conv_w[j], ab_conv_b[j], ab_ln_g[j], ab_ln_b[j], ab_w_out[j])
            x_lat = _conv_fourier_layer(x_lat, norm1_g[i], sh1, sc1, g1, True, *ab, tl=512, tq=256)
            if ctx_out:
                x_ctx = _conv_fourier_layer(x_ctx, norm1_g[i], csh1, csc1, cg1, False, *ab,
                                            tl=n_ctx, tq=n_ctx)
        else:
            assert not ctx_out, "context-stream attention output is only needed for deeper stacks"
            w_in_ext, wq, wkv = _mla_weights(mla_w_in[j], mla_w_uq[j], mla_w_ukv[j])
            cos_t, sin_t = _rope_tables(L)
            g = norm1_g[i].reshape(1, d)
            q_g = mla_q_norm_g[j].reshape(1, Q_LORA)
            kv_g = mla_kv_norm_g[j].reshape(1, KV_LORA)
            q, k_lat, v_lat = _mla_proj(x_lat, g, sh1, sc1, True, w_in_ext, q_g, kv_g, wq, wkv,
                                        cos_t, sin_t, True, 512)
            k_ctx, v_ctx = _mla_proj(x_ctx, g, csh1, csc1, False,
                                     w_in_ext[:, :Q_LORA + KV_LORA + HEAD_PAD], None, kv_g, None, wkv,
                                     None, None, False, n_ctx)
            x_lat = _mla_attn(q, k_ctx, k_lat, v_ctx, v_lat, x_lat, g1, mla_w_o[j].astype(BF16), 512)
        moe = (i, moe_w_router[i], moe_w1, moe_w3, moe_w2)
        x_lat = _ec_moe_layer(x_lat, norm2_g[i], sh2, sc2, g2, True, *moe, tl=1024, nb=2, gs=4,
                              final_g=final_g if last else None)
        if ctx_out:
            x_ctx = _ec_moe_layer(x_ctx, norm2_g[i], csh2, csc2, cg2, False, *moe, tl=n_ctx, nb=n, gs=n)
    return x_lat
```
